```python
import math
import jax, jax.numpy as jnp
from jax import lax
import numpy as np

D_MODEL = 2048
BATCH = 4
SEQ = 2048
DEPTH = 2
DEC_BATCH = 128
DEC_SEQ = 1
PAST_LEN = 16384
PAGE_SIZE = 128

N_EVEN = (DEPTH + 1) // 2
N_ODD = DEPTH // 2
D_A = D_MODEL // 2
D_B = D_MODEL // 2
K_A = 31
K_B = 3
D_C = D_MODEL
H_C = 8
DH_C = D_C // H_C
CHUNK = 128
D_FF = 11 * D_MODEL // 4
D_AB_IN = 2 * D_A + 3 * D_B
EPS = 1e-6

kernel_name = "macaron_conv_gmlp_hybrid_step"


def rmsnorm(x, g):
    xf = x.astype(jnp.float32)
    y = xf * lax.rsqrt(jnp.mean(xf * xf, axis=-1, keepdims=True) + EPS)
    return (y * g.astype(jnp.float32)).astype(x.dtype)


def layernorm(x, g, b):
    xf = x.astype(jnp.float32)
    mu = jnp.mean(xf, axis=-1, keepdims=True)
    xc = xf - mu
    y = xc * lax.rsqrt(jnp.mean(xc * xc, axis=-1, keepdims=True) + EPS)
    return (y * g.astype(jnp.float32) + b.astype(jnp.float32)).astype(x.dtype)


def swiglu(x, w1, w3, w2):
    return (jax.nn.silu(x @ w1) * (x @ w3)) @ w2


def causal_depthwise(rows, prefix, w):
    full = jnp.concatenate([prefix.astype(rows.dtype), rows], axis=1)
    out = lax.conv_general_dilated(
        full, w[:, None, :].astype(rows.dtype), window_strides=(1,), padding='VALID',
        dimension_numbers=('NWC', 'WIO', 'NWC'), feature_group_count=rows.shape[-1])
    return out, full[:, -(w.shape[0] - 1):]


def conv_mixers(h, st_a, st_b, w_in, a_conv_w, a_conv_b, a_ln_g, a_ln_b, b_conv_w, w_out):
    p = h @ w_in
    pa, ga, bg, cg, hb = jnp.split(
        p, [D_A, 2 * D_A, 2 * D_A + D_B, 2 * D_A + 2 * D_B], axis=-1)
    a = pa * jax.nn.sigmoid(ga)
    a_c, st_a_new = causal_depthwise(a, st_a, a_conv_w)
    a_out = jax.nn.silu(layernorm(a_c + a_conv_b, a_ln_g, a_ln_b))
    b_c, st_b_new = causal_depthwise(cg * hb, st_b, b_conv_w)
    b_out = bg * b_c
    y = jnp.concatenate([a_out, b_out], axis=-1) @ w_out
    return y, st_a_new, st_b_new


def chunk_gmlp(h, w_in, b_in, ln_g, ln_b, w_s, b_s, w_out, chunk_len):
    bsz, t, _ = h.shape
    z = jax.nn.gelu(h @ w_in + b_in, approximate=False)
    u, v = jnp.split(z, 2, axis=-1)
    v = layernorm(v, ln_g, ln_b)
    L = chunk_len
    n = t // L
    mask = jnp.tril(jnp.ones((L, L), dtype=bool))
    ws = jnp.where(mask[None], w_s[:, :L, :L], 0).astype(v.dtype)
    vc = v.reshape(bsz, n, L, H_C, DH_C)
    s = jnp.einsum('hts,bnshd->bnthd', ws, vc) + b_s[:, :L].T[None, None, :, :, None].astype(v.dtype)
    y = (u * s.reshape(bsz, t, D_C)) @ w_out
    return y, v


def trunk(x, conv_a_prev, conv_b_prev, chunk_len, norm_g, ffn_w1, ffn_w3, ffn_w2,
          ab_w_in, a_conv_w, a_conv_b, a_ln_g, a_ln_b, b_conv_w, ab_w_out,
          c_w_in, c_b_in, c_ln_g, c_ln_b, c_w_s, c_b_s, c_w_out, final_g):
    new_a, new_b, new_v = [], [], []
    for i in range(DEPTH):
        x = x + 0.5 * swiglu(rmsnorm(x, norm_g[i, 0]), ffn_w1[i, 0], ffn_w3[i, 0], ffn_w2[i, 0])
        hn = rmsnorm(x, norm_g[i, 1])
        if i % 2 == 0:
            j = i // 2
            m, sa, sb = conv_mixers(hn, conv_a_prev[j], conv_b_prev[j], ab_w_in[j], a_conv_w[j],
                                    a_conv_b[j], a_ln_g[j], a_ln_b[j], b_conv_w[j], ab_w_out[j])
            new_a.append(sa)
            new_b.append(sb)
        else:
            j = i // 2
            m, v = chunk_gmlp(hn, c_w_in[j], c_b_in[j], c_ln_g[j], c_ln_b[j], c_w_s[j],
                              c_b_s[j], c_w_out[j], chunk_len)
            new_v.append(v)
        x = x + m
        x = x + 0.5 * swiglu(rmsnorm(x, norm_g[i, 2]), ffn_w1[i, 1], ffn_w3[i, 1], ffn_w2[i, 1])
    return rmsnorm(x, final_g), jnp.stack(new_a), jnp.stack(new_b), jnp.stack(new_v)


def setup_inputs(seed: int = 0) -> dict:
    key = jax.random.key(seed)
    ks = jax.random.split(key, 24)
    f32 = jnp.float32
    nrm = lambda k, shape, s: jax.random.normal(k, shape, f32) * s
    return {
        "x_prompt": nrm(ks[0], (BATCH, SEQ, D_MODEL), 1.0),
        "x_sample": nrm(ks[1], (DEC_BATCH, DEC_SEQ, D_MODEL), 1.0),
        "state_conv_a": nrm(ks[2], (N_EVEN, DEC_BATCH, K_A - 1, D_A), 0.5),
        "state_conv_b": nrm(ks[3], (N_EVEN, DEC_BATCH, K_B - 1, D_B), 0.5),
        "norm_g": 1.0 + nrm(ks[4], (DEPTH, 3, D_MODEL), 0.01),
        "ffn_w1": nrm(ks[5], (DEPTH, 2, D_MODEL, D_FF), D_MODEL ** -0.5),
        "ffn_w3": nrm(ks[6], (DEPTH, 2, D_MODEL, D_FF), D_MODEL ** -0.5),
        "ffn_w2": nrm(ks[7], (DEPTH, 2, D_FF, D_MODEL), D_FF ** -0.5),
        "ab_w_in": nrm(ks[8], (N_EVEN, D_MODEL, D_AB_IN), D_MODEL ** -0.5),
        "a_conv_w": nrm(ks[9], (N_EVEN, K_A, D_A), K_A ** -0.5),
        "a_conv_b": nrm(ks[10], (N_EVEN, D_A), 0.01),
        "a_ln_g": 1.0 + nrm(ks[11], (N_EVEN, D_A), 0.01),
        "a_ln_b": nrm(ks[12], (N_EVEN, D_A), 0.01),
        "b_conv_w": nrm(ks[13], (N_EVEN, K_B, D_B), K_B ** -0.5),
        "ab_w_out": nrm(ks[14], (N_EVEN, D_A + D_B, D_MODEL), (D_A + D_B) ** -0.5),
        "c_w_in": nrm(ks[15], (N_ODD, D_MODEL, 2 * D_C), D_MODEL ** -0.5),
        "c_b_in": nrm(ks[16], (N_ODD, 2 * D_C), 0.01),
        "c_ln_g": 1.0 + nrm(ks[17], (N_ODD, D_C), 0.01),
        "c_ln_b": nrm(ks[18], (N_ODD, D_C), 0.01),
        "c_w_s": nrm(ks[19], (N_ODD, H_C, CHUNK, CHUNK), CHUNK ** -0.5),
        "c_b_s": 1.0 + nrm(ks[20], (N_ODD, H_C, CHUNK), 0.01),
        "c_w_out": nrm(ks[21], (N_ODD, D_C, D_MODEL), D_C ** -0.5),
        "final_g": 1.0 + nrm(ks[22], (D_MODEL,), 0.01),
    }


def reference(x_prompt, x_sample, state_conv_a, state_conv_b, norm_g, ffn_w1, ffn_w3, ffn_w2,
              ab_w_in, a_conv_w, a_conv_b, a_ln_g, a_ln_b, b_conv_w, ab_w_out,
              c_w_in, c_b_in, c_ln_g, c_ln_b, c_w_s, c_b_s, c_w_out, final_g):
    bsz = x_prompt.shape[0]
    zero_a = jnp.zeros((N_EVEN, bsz, K_A - 1, D_A), x_prompt.dtype)
    zero_b = jnp.zeros((N_EVEN, bsz, K_B - 1, D_B), x_prompt.dtype)
    y_prompt, new_conv_a_prompt, new_conv_b_prompt, _ = trunk(
        x_prompt, zero_a, zero_b, CHUNK, norm_g, ffn_w1, ffn_w3, ffn_w2,
        ab_w_in, a_conv_w, a_conv_b, a_ln_g, a_ln_b, b_conv_w, ab_w_out,
        c_w_in, c_b_in, c_ln_g, c_ln_b, c_w_s, c_b_s, c_w_out, final_g)
    y_sample, new_conv_a_sample, new_conv_b_sample, new_chunk_v_sample = trunk(
        x_sample, state_conv_a, state_conv_b, x_sample.shape[1], norm_g, ffn_w1, ffn_w3, ffn_w2,
        ab_w_in, a_conv_w, a_conv_b, a_ln_g, a_ln_b, b_conv_w, ab_w_out,
        c_w_in, c_b_in, c_ln_g, c_ln_b, c_w_s, c_b_s, c_w_out, final_g)
    return (y_prompt, y_sample, new_conv_a_prompt, new_conv_b_prompt,
            new_conv_a_sample, new_conv_b_sample, new_chunk_v_sample)
```

```python
import functools

import jax
import jax.numpy as jnp
from jax import lax
from jax.experimental import pallas as pl
from jax.experimental.pallas import tpu as pltpu

F32 = jnp.float32
BF16 = jnp.bfloat16

D_MODEL = 2048
D_FF = 5632
D_A = 1024
D_B = 1024
K_A = 31
K_B = 3
D_C = 2048
H_C = 8
DH_C = D_C // H_C
CHUNK = 128
EPS = 1e-6

VMEM_LIMIT_BYTES = 56 * 1024 * 1024

TM = 1040
NORM_ROWS = 208
TF = 512
TN = 1024
CONV_T = 256
CONV_RB = 64
A_HALO = 32
B_HALO = 8
SAMPLE_BS = 32


def _params(sem):
    return pltpu.CompilerParams(dimension_semantics=sem, vmem_limit_bytes=VMEM_LIMIT_BYTES)


def _rms_rows(x, g):
    ms = jnp.mean(x * x, axis=-1, keepdims=True)
    return x * lax.rsqrt(ms + EPS) * g


def _layernorm_rows(x, g, b):
    mu = jnp.mean(x, axis=-1, keepdims=True)
    xc = x - mu
    var = jnp.mean(xc * xc, axis=-1, keepdims=True)
    return xc * lax.rsqrt(var + EPS) * g + b


def _norm_to_bf16(src_ref, g_ref, dst_ref, rows):
    def body(c, carry):
        r = pl.ds(pl.multiple_of(c * NORM_ROWS, NORM_ROWS), NORM_ROWS)
        dst_ref[r, :] = _rms_rows(src_ref[r, :], g_ref[...]).astype(BF16)
        return carry
    lax.fori_loop(0, rows // NORM_ROWS, body, 0)


def _ffn_kernel(x_hbm, g_ref, w1_ref, w3_ref, w2_ref, fg_ref, o_ref, xn_ref, sem, *, final_norm):
    i = pl.program_id(0)
    j = pl.program_id(1)

    @pl.when(j == 0)
    def _():
        cp = pltpu.make_async_copy(x_hbm.at[pl.ds(i * TM, TM)], o_ref, sem)
        cp.start()
        cp.wait()
        _norm_to_bf16(o_ref, g_ref, xn_ref, TM)

    xn = xn_ref[...]
    h1 = jnp.dot(xn, w1_ref[...], preferred_element_type=F32)
    h3 = jnp.dot(xn, w3_ref[...], preferred_element_type=F32)
    h = (0.5 * (h1 * jax.nn.sigmoid(h1)) * h3).astype(BF16)
    for n in range(D_MODEL // TF):
        cs = slice(n * TF, (n + 1) * TF)
        o_ref[:, cs] += jnp.dot(h, w2_ref[:, cs], preferred_element_type=F32)

    if final_norm:
        @pl.when(j == pl.num_programs(1) - 1)
        def _():
            def body(c, carry):
                r = pl.ds(pl.multiple_of(c * NORM_ROWS, NORM_ROWS), NORM_ROWS)
                o_ref[r, :] = _rms_rows(o_ref[r, :], fg_ref[...])
                return carry
            lax.fori_loop(0, TM // NORM_ROWS, body, 0)


def _ffn(x, g, w1, w3, w2, final_g, final_norm):
    rows = x.shape[0]
    return pl.pallas_call(
        functools.partial(_ffn_kernel, final_norm=final_norm),
        out_shape=jax.ShapeDtypeStruct((rows, D_MODEL), F32),
        grid=(rows // TM, D_FF // TF),
        in_specs=[
            pl.BlockSpec(memory_space=pl.ANY),
            pl.BlockSpec((1, D_MODEL), lambda i, j: (0, 0)),
            pl.BlockSpec((D_MODEL, TF), lambda i, j: (0, j)),
            pl.BlockSpec((D_MODEL, TF), lambda i, j: (0, j)),
            pl.BlockSpec((TF, D_MODEL), lambda i, j: (j, 0)),
            pl.BlockSpec((1, D_MODEL), lambda i, j: (0, 0)),
        ],
        out_specs=pl.BlockSpec((TM, D_MODEL), lambda i, j: (i, 0)),
        scratch_shapes=[pltpu.VMEM((TM, D_MODEL), BF16), pltpu.SemaphoreType.DMA(())],
        compiler_params=_params(("arbitrary", "arbitrary")),
        name="ffn",
    )(x, g, w1, w3, w2, final_g)


def _norm_proj_kernel(x_ref, g_ref, w_ref, b_ref, o_ref, xn_ref, *, gelu):
    @pl.when(pl.program_id(1) == 0)
    def _():
        _norm_to_bf16(x_ref, g_ref, xn_ref, TM)

    y = jnp.dot(xn_ref[...], w_ref[...], preferred_element_type=F32)
    if gelu:
        y = y + b_ref[...]
        y = 0.5 * y * (1.0 + lax.erf(y * (2.0 ** -0.5)))
    o_ref[...] = y


def _norm_proj(x, g, w, b, gelu):
    rows = x.shape[0]
    n = w.shape[1]
    return pl.pallas_call(
        functools.partial(_norm_proj_kernel, gelu=gelu),
        out_shape=jax.ShapeDtypeStruct((rows, n), F32),
        grid=(rows // TM, n // TN),
        in_specs=[
            pl.BlockSpec((TM, D_MODEL), lambda i, j: (i, 0)),
            pl.BlockSpec((1, D_MODEL), lambda i, j: (0, 0)),
            pl.BlockSpec((D_MODEL, TN), lambda i, j: (0, j)),
            pl.BlockSpec((1, TN), lambda i, j: (0, j)),
        ],
        out_specs=pl.BlockSpec((TM, TN), lambda i, j: (i, j)),
        scratch_shapes=[pltpu.VMEM((TM, D_MODEL), BF16)],
        compiler_params=_params(("arbitrary", "arbitrary")),
        name="norm_proj_gelu" if gelu else "norm_proj",
    )(x, g, w, b)


def _proj_residual_kernel(h_ref, w_ref, x_ref, o_ref):
    o_ref[...] = x_ref[...] + jnp.dot(h_ref[...], w_ref[...], preferred_element_type=F32)


def _proj_residual(h, w, x):
    rows, k = h.shape
    return pl.pallas_call(
        _proj_residual_kernel,
        out_shape=jax.ShapeDtypeStruct((rows, D_MODEL), F32),
        grid=(rows // TM, D_MODEL // TN),
        in_specs=[
            pl.BlockSpec((TM, k), lambda i, j: (i, 0)),
            pl.BlockSpec((k, TN), lambda i, j: (0, j)),
            pl.BlockSpec((TM, TN), lambda i, j: (i, j)),
        ],
        out_specs=pl.BlockSpec((TM, TN), lambda i, j: (i, j)),
        compiler_params=_params(("arbitrary", "arbitrary")),
        name="proj_residual",
    )(h, w, x)


def _conv_prompt_kernel(p_ref, wa_ref, ba_ref, lg_ref, lb_ref, wb_ref,
                        h_ref, nsa_ref, nsb_ref, abuf, bbuf, cbuf):
    t = pl.program_id(1)

    @pl.when(t == 0)
    def _():
        abuf[0:A_HALO, :] = jnp.zeros((A_HALO, D_A), F32)
        bbuf[0:B_HALO, :] = jnp.zeros((B_HALO, D_B), F32)

    abuf[A_HALO:A_HALO + CONV_T, :] = p_ref[:, 0:D_A] * jax.nn.sigmoid(p_ref[:, D_A:2 * D_A])
    bbuf[B_HALO:B_HALO + CONV_T, :] = (p_ref[:, 2 * D_A + D_B:2 * D_A + 2 * D_B]
                                       * p_ref[:, 2 * D_A + 2 * D_B:2 * D_A + 3 * D_B])

    a0 = A_HALO - (K_A - 1)
    for c in range(D_A // 128):
        cs = slice(c * 128, (c + 1) * 128)
        for r in range(CONV_T // CONV_RB):
            r0 = r * CONV_RB
            acc = wa_ref[0:1, cs] * abuf[a0 + r0:a0 + r0 + CONV_RB, cs]
            for k in range(1, K_A):
                acc = acc + wa_ref[k:k + 1, cs] * abuf[a0 + k + r0:a0 + k + r0 + CONV_RB, cs]
            cbuf[r0:r0 + CONV_RB, cs] = acc + ba_ref[:, cs]

    a_ln = _layernorm_rows(cbuf[...], lg_ref[...], lb_ref[...])
    h_ref[:, 0:D_A] = (a_ln * jax.nn.sigmoid(a_ln)).astype(BF16)

    b0 = B_HALO - (K_B - 1)
    b_c = wb_ref[0:1, :] * bbuf[b0:b0 + CONV_T, :]
    for k in range(1, K_B):
        b_c = b_c + wb_ref[k:k + 1, :] * bbuf[b0 + k:b0 + k + CONV_T, :]
    h_ref[:, D_A:D_A + D_B] = (p_ref[:, 2 * D_A:2 * D_A + D_B] * b_c).astype(BF16)

    nsa_ref[0] = abuf[CONV_T + a0:CONV_T + A_HALO, :]
    nsb_ref[0] = bbuf[CONV_T + b0:CONV_T + B_HALO, :]
    abuf[0:A_HALO, :] = abuf[CONV_T:CONV_T + A_HALO, :]
    bbuf[0:B_HALO, :] = bbuf[CONV_T:CONV_T + B_HALO, :]


def _conv_prompt(p, wa, ba, lg, lb, wb, batch, seq):
    rows = batch * seq
    nt = seq // CONV_T
    return pl.pallas_call(
        _conv_prompt_kernel,
        out_shape=(jax.ShapeDtypeStruct((rows, D_A + D_B), BF16),
                   jax.ShapeDtypeStruct((batch, K_A - 1, D_A), F32),
                   jax.ShapeDtypeStruct((batch, K_B - 1, D_B), F32)),
        grid=(batch, nt),
        in_specs=[
            pl.BlockSpec((CONV_T, p.shape[1]), lambda b, t: (b * nt + t, 0)),
            pl.BlockSpec((K_A, D_A), lambda b, t: (0, 0)),
            pl.BlockSpec((1, D_A), lambda b, t: (0, 0)),
            pl.BlockSpec((1, D_A), lambda b, t: (0, 0)),
            pl.BlockSpec((1, D_A), lambda b, t: (0, 0)),
            pl.BlockSpec((K_B, D_B), lambda b, t: (0, 0)),
        ],
        out_specs=(pl.BlockSpec((CONV_T, D_A + D_B), lambda b, t: (b * nt + t, 0)),
                   pl.BlockSpec((1, K_A - 1, D_A), lambda b, t: (b, 0, 0)),
                   pl.BlockSpec((1, K_B - 1, D_B), lambda b, t: (b, 0, 0))),
        scratch_shapes=[pltpu.VMEM((A_HALO + CONV_T, D_A), F32),
                        pltpu.VMEM((B_HALO + CONV_T, D_B), F32),
                        pltpu.VMEM((CONV_T, D_A), F32)],
        compiler_params=_params(("arbitrary", "arbitrary")),
        name="conv_prompt",
    )(p, wa, ba, lg, lb, wb)


def _conv_sample_kernel(p_ref, sa_ref, sb_ref, wa_ref, ba_ref, lg_ref, lb_ref, wb_ref,
                        h_ref, anew_ref, bnew_ref):
    a = p_ref[:, 0:D_A] * jax.nn.sigmoid(p_ref[:, D_A:2 * D_A])
    acc = wa_ref[K_A - 1:K_A, :] * a
    for k in range(K_A - 1):
        acc = acc + wa_ref[k:k + 1, :] * sa_ref[:, k, :]
    a_ln = _layernorm_rows(acc + ba_ref[...], lg_ref[...], lb_ref[...])
    h_ref[:, 0:D_A] = (a_ln * jax.nn.sigmoid(a_ln)).astype(BF16)
    anew_ref[...] = a

    ch = (p_ref[:, 2 * D_A + D_B:2 * D_A + 2 * D_B]
          * p_ref[:, 2 * D_A + 2 * D_B:2 * D_A + 3 * D_B])
    b_c = wb_ref[K_B - 1:K_B, :] * ch
    for k in range(K_B - 1):
        b_c = b_c + wb_ref[k:k + 1, :] * sb_ref[:, k, :]
    h_ref[:, D_A:D_A + D_B] = (p_ref[:, 2 * D_A:2 * D_A + D_B] * b_c).astype(BF16)
    bnew_ref[...] = ch


def _conv_sample(p, sa, sb, wa, ba, lg, lb, wb, row0):
    nb = sa.shape[0]
    blk0 = row0 // SAMPLE_BS
    return pl.pallas_call(
        _conv_sample_kernel,
        out_shape=(jax.ShapeDtypeStruct((nb, D_A + D_B), BF16),
                   jax.ShapeDtypeStruct((nb, D_A), F32),
                   jax.ShapeDtypeStruct((nb, D_B), F32)),
        grid=(nb // SAMPLE_BS,),
        in_specs=[
            pl.BlockSpec((SAMPLE_BS, p.shape[1]), lambda s: (blk0 + s, 0)),
            pl.BlockSpec((SAMPLE_BS, K_A - 1, D_A), lambda s: (s, 0, 0)),
            pl.BlockSpec((SAMPLE_BS, K_B - 1, D_B), lambda s: (s, 0, 0)),
            pl.BlockSpec((K_A, D_A), lambda s: (0, 0)),
            pl.BlockSpec((1, D_A), lambda s: (0, 0)),
            pl.BlockSpec((1, D_A), lambda s: (0, 0)),
            pl.BlockSpec((1, D_A), lambda s: (0, 0)),
            pl.BlockSpec((K_B, D_B), lambda s: (0, 0)),
        ],
        out_specs=(pl.BlockSpec((SAMPLE_BS, D_A + D_B), lambda s: (s, 0)),
                   pl.BlockSpec((SAMPLE_BS, D_A), lambda s: (s, 0)),
                   pl.BlockSpec((SAMPLE_BS, D_B), lambda s: (s, 0))),
        compiler_params=_params(("arbitrary",)),
        name="conv_sample",
    )(p, sa, sb, wa, ba, lg, lb, wb)


def _gate_kernel(z_ref, lg_ref, lb_ref, ws_ref, bs_ref, o_ref, v_ref, *, n_prompt_chunks):
    i = pl.program_id(0)
    is_sample = i >= n_prompt_chunks
    vn = _layernorm_rows(z_ref[:, D_C:2 * D_C], lg_ref[...], lb_ref[...])
    vb = vn.astype(BF16)
    row = lax.broadcasted_iota(jnp.int32, (CHUNK, CHUNK), 0)
    col = lax.broadcasted_iota(jnp.int32, (CHUNK, CHUNK), 1)
    keep = (col <= row) & (col >= jnp.where(is_sample, row, 0))
    for h in range(H_C):
        hs = slice(h * DH_C, (h + 1) * DH_C)
        w = jnp.where(keep, ws_ref[0, h], 0.0).astype(BF16)
        s = jnp.dot(w, vb[:, hs], preferred_element_type=F32) + bs_ref[0][:, h:h + 1]
        o_ref[:, hs] = (z_ref[:, hs] * s).astype(BF16)

    @pl.when(i == pl.num_programs(0) - 1)
    def _():
        v_ref[...] = vn


def _gate(z, lg, lb, ws_all, bs_all, n_prompt_chunks):
    rows = z.shape[0]
    sel = lambda i: jnp.where(i >= n_prompt_chunks, 1, 0)
    return pl.pallas_call(
        functools.partial(_gate_kernel, n_prompt_chunks=n_prompt_chunks),
        out_shape=(jax.ShapeDtypeStruct((rows, D_C), BF16),
                   jax.ShapeDtypeStruct((CHUNK, D_C), F32)),
        grid=(rows // CHUNK,),
        in_specs=[
            pl.BlockSpec((CHUNK, 2 * D_C), lambda i: (i, 0)),
            pl.BlockSpec((1, D_C), lambda i: (0, 0)),
            pl.BlockSpec((1, D_C), lambda i: (0, 0)),
            pl.BlockSpec((1, H_C, CHUNK, CHUNK), lambda i: (sel(i), 0, 0, 0)),
            pl.BlockSpec((1, CHUNK, H_C), lambda i: (sel(i), 0, 0)),
        ],
        out_specs=(pl.BlockSpec((CHUNK, D_C), lambda i: (i, 0)),
                   pl.BlockSpec((CHUNK, D_C), lambda i: (0, 0))),
        compiler_params=_params(("arbitrary",)),
        name="gate",
    )(z, lg, lb, ws_all, bs_all)


def kernel(x_prompt, x_sample, state_conv_a, state_conv_b, norm_g, ffn_w1, ffn_w3, ffn_w2,
           ab_w_in, a_conv_w, a_conv_b, a_ln_g, a_ln_b, b_conv_w, ab_w_out,
           c_w_in, c_b_in, c_ln_g, c_ln_b, c_w_s, c_b_s, c_w_out, final_g):
    batch, seq, _ = x_prompt.shape
    nb, dec_seq, _ = x_sample.shape
    depth = norm_g.shape[0]
    n_prompt = batch * seq
    assert dec_seq == 1 and nb == CHUNK and seq % CONV_T == 0 and n_prompt % CHUNK == 0
    assert (n_prompt + nb) % TM == 0

    x = jnp.concatenate([x_prompt.reshape(n_prompt, D_MODEL), x_sample.reshape(nb, D_MODEL)], axis=0)
    row = lambda v: v.reshape(1, -1)
    fg = row(final_g)

    new_a_p, new_b_p, new_a_s, new_b_s, new_v_s = [], [], [], [], []
    for i in range(depth):
        x = _ffn(x, row(norm_g[i, 0]), ffn_w1[i, 0].astype(BF16), ffn_w3[i, 0].astype(BF16),
                 ffn_w2[i, 0].astype(BF16), fg, False)
        j = i // 2
        if i % 2 == 0:
            p = _norm_proj(x, row(norm_g[i, 1]), ab_w_in[j].astype(BF16),
                           jnp.zeros((1, ab_w_in.shape[2]), F32), False)
            conv_w = (a_conv_w[j], row(a_conv_b[j]), row(a_ln_g[j]), row(a_ln_b[j]), b_conv_w[j])
            h_p, sa_p, sb_p = _conv_prompt(p, *conv_w, batch, seq)
            h_s, a_new, b_new = _conv_sample(p, state_conv_a[j], state_conv_b[j], *conv_w, n_prompt)
            h = jnp.concatenate([h_p, h_s], axis=0)
            new_a_p.append(sa_p)
            new_b_p.append(sb_p)
            new_a_s.append(jnp.concatenate([state_conv_a[j][:, 1:], a_new[:, None, :]], axis=1))
            new_b_s.append(jnp.concatenate([state_conv_b[j][:, 1:], b_new[:, None, :]], axis=1))
            x = _proj_residual(h, ab_w_out[j].astype(BF16), x)
        else:
            z = _norm_proj(x, row(norm_g[i, 1]), c_w_in[j].astype(BF16), row(c_b_in[j]), True)
            w_s = c_w_s[j][:, :CHUNK, :CHUNK]
            ws_all = jnp.stack([w_s, jnp.broadcast_to(w_s[:, :1, :1], w_s.shape)])
            b_s = c_b_s[j][:, :CHUNK].T
            bs_all = jnp.stack([b_s, jnp.broadcast_to(b_s[:1], b_s.shape)])
            h, v_s = _gate(z, row(c_ln_g[j]), row(c_ln_b[j]), ws_all, bs_all, n_prompt // CHUNK)
            new_v_s.append(v_s.reshape(nb, 1, D_C))
            x = _proj_residual(h, c_w_out[j].astype(BF16), x)
        x = _ffn(x, row(norm_g[i, 2]), ffn_w1[i, 1].astype(BF16), ffn_w3[i, 1].astype(BF16),
                 ffn_w2[i, 1].astype(BF16), fg, i == depth - 1)

    return (x[:n_prompt].reshape(batch, seq, D_MODEL), x[n_prompt:].reshape(nb, 1, D_MODEL),
            jnp.stack(new_a_p), jnp.stack(new_b_p), jnp.stack(new_a_s), jnp.stack(new_b_s),
            jnp.stack(new_v_s))
```

```python
import functools

import jax
import jax.numpy as jnp
from jax import lax
from jax.experimental import pallas as pl
from jax.experimental.pallas import tpu as pltpu

F32 = jnp.float32
BF16 = jnp.bfloat16

D_MODEL = 2048
D_FF = 5632
D_A = 1024
D_B = 1024
K_A = 31
K_B = 3
D_C = 2048
H_C = 8
DH_C = D_C // H_C
CHUNK = 128
EPS = 1e-6

VMEM_LIMIT_BYTES = 56 * 1024 * 1024

TM = 1040
NORM_ROWS = 208
TF = 512
TN = 1024
CONV_T = 256
CONV_RB = 64
A_HALO = 32
B_HALO = 8
SAMPLE_BS = 32


def _params(sem):
    return pltpu.CompilerParams(dimension_semantics=sem, vmem_limit_bytes=VMEM_LIMIT_BYTES)


def _rms_rows(x, g):
    ms = jnp.mean(x * x, axis=-1, keepdims=True)
    return x * lax.rsqrt(ms + EPS) * g


def _layernorm_rows(x, g, b):
    mu = jnp.mean(x, axis=-1, keepdims=True)
    xc = x - mu
    var = jnp.mean(xc * xc, axis=-1, keepdims=True)
    return xc * lax.rsqrt(var + EPS) * g + b


def _norm_chunks(src_ref, g_ref, dst_ref, dtype):
    def body(c, carry):
        r = pl.ds(pl.multiple_of(c * NORM_ROWS, NORM_ROWS), NORM_ROWS)
        dst_ref[r, :] = _rms_rows(src_ref[r, :], g_ref[...]).astype(dtype)
        return carry
    lax.fori_loop(0, TM // NORM_ROWS, body, 0)


def _tile_pieces(tile, seg_rows):
    lo, hi = tile * TM, (tile + 1) * TM
    pieces, base = [], 0
    for k, n in enumerate(seg_rows):
        a, b = max(lo, base), min(hi, base + n)
        if a < b:
            pieces.append((k, a - base, a - lo, b - a))
        base += n
    return pieces


def _ffn_kernel(*refs, in_rows, out_rows, convert_next, emit_norm, final_norm):
    n_in, n_out = len(in_rows), len(out_rows)
    it = iter(refs)
    x_srcs = [next(it) for _ in range(n_in)]
    g_ref, w1_ref, w3_ref, w2_ref, post_g_ref = (next(it) for _ in range(5))
    next_w = [next(it) for _ in range(3)] if convert_next else []
    o_dsts = [next(it) for _ in range(n_out)]
    xn_out = next(it) if emit_norm else None
    next_w_out = [next(it) for _ in range(3)] if convert_next else []
    acc_ref, xn_ref, load_sem, store_sem, xn_sem = (next(it) for _ in range(5))

    i = pl.program_id(0)
    j = pl.program_id(1)
    n_i = pl.num_programs(0)
    n_j = pl.num_programs(1)
    n_tiles = sum(in_rows) // TM

    def load_copies(t):
        return [pltpu.make_async_copy(x_srcs[k].at[pl.ds(r0, n)],
                                      acc_ref.at[t % 2, pl.ds(d0, n)], load_sem.at[t % 2])
                for k, r0, d0, n in _tile_pieces(t, in_rows)]

    def store_copies(t):
        return [pltpu.make_async_copy(acc_ref.at[t % 2, pl.ds(d0, n)],
                                      o_dsts[k].at[pl.ds(r0, n)], store_sem.at[t % 2])
                for k, r0, d0, n in _tile_pieces(t, out_rows)]

    def xn_copy(t):
        return pltpu.make_async_copy(xn_ref, xn_out.at[pl.ds(t * TM, TM)], xn_sem)

    def on_tile(tile, fn):
        for t in range(n_tiles):
            @pl.when(tile == t)
            def _(t=t):
                fn(t)

    acc = acc_ref.at[lax.rem(i, 2)]

    @pl.when(j == 0)
    def _():
        @pl.when(i == 0)
        def _():
            for cp in load_copies(0):
                cp.start()
        on_tile(i, lambda t: [cp.wait() for cp in load_copies(t)])
        if emit_norm:
            @pl.when(i > 0)
            def _():
                on_tile(i - 1, lambda t: xn_copy(t).wait())
        _norm_chunks(acc, g_ref, xn_ref, BF16)

    @pl.when(j == 2)
    def _():
        @pl.when(i > 0)
        def _():
            on_tile(i - 1, lambda t: [cp.wait() for cp in store_copies(t)])

        @pl.when(i + 1 < n_i)
        def _():
            on_tile(i + 1, lambda t: [cp.start() for cp in load_copies(t)])

    xn = xn_ref[...]
    h1 = jnp.dot(xn, w1_ref[...], preferred_element_type=F32)
    h3 = jnp.dot(xn, w3_ref[...], preferred_element_type=F32)
    h = (0.5 * (h1 * jax.nn.sigmoid(h1)) * h3).astype(BF16)
    for n in range(D_MODEL // TF):
        cs = slice(n * TF, (n + 1) * TF)
        acc[:, cs] += jnp.dot(h, w2_ref[:, cs], preferred_element_type=F32)

    if convert_next:
        for src, dst in zip(next_w, next_w_out):
            dst[...] = src[...].astype(BF16)

    @pl.when(j == n_j - 1)
    def _():
        if final_norm:
            _norm_chunks(acc, post_g_ref, acc, F32)
        on_tile(i, lambda t: [cp.start() for cp in store_copies(t)])
        if emit_norm:
            _norm_chunks(acc, post_g_ref, xn_ref, BF16)
            on_tile(i, lambda t: xn_copy(t).start())

        @pl.when(i == n_i - 1)
        def _():
            on_tile(i, lambda t: [cp.wait() for cp in store_copies(t)])
            if emit_norm:
                on_tile(i, lambda t: xn_copy(t).wait())


def _ffn(xs, g, w1, w3, w2, post_g, out_rows, next_w=None, emit_norm=False, final_norm=False):
    in_rows = tuple(a.shape[0] for a in xs)
    rows = sum(in_rows)
    assert rows == sum(out_rows) and rows % TM == 0 and D_FF % TF == 0
    n_i, n_j = rows // TM, D_FF // TF
    assert n_j > 2 and D_MODEL % n_i == 0
    convert_next = next_w is not None
    any_spec = pl.BlockSpec(memory_space=pl.ANY)
    row_spec = pl.BlockSpec((1, D_MODEL), lambda i, j: (0, 0))
    up_piece = pl.BlockSpec((D_MODEL // n_i, TF), lambda i, j: (i, j))
    down_piece = pl.BlockSpec((TF, D_MODEL // n_i), lambda i, j: (j, i))

    in_specs = [any_spec] * len(xs) + [
        row_spec,
        pl.BlockSpec((D_MODEL, TF), lambda i, j: (0, j)),
        pl.BlockSpec((D_MODEL, TF), lambda i, j: (0, j)),
        pl.BlockSpec((TF, D_MODEL), lambda i, j: (j, 0)),
        row_spec,
    ]
    args = list(xs) + [g, w1, w3, w2, post_g]
    out_shape = [jax.ShapeDtypeStruct((n, D_MODEL), F32) for n in out_rows]
    out_specs = [any_spec] * len(out_rows)
    if emit_norm:
        out_shape.append(jax.ShapeDtypeStruct((rows, D_MODEL), BF16))
        out_specs.append(any_spec)
    if convert_next:
        in_specs += [up_piece, up_piece, down_piece]
        args += list(next_w)
        out_shape += [jax.ShapeDtypeStruct(w.shape, BF16) for w in next_w]
        out_specs += [up_piece, up_piece, down_piece]

    res = pl.pallas_call(
        functools.partial(_ffn_kernel, in_rows=in_rows, out_rows=tuple(out_rows),
                          convert_next=convert_next, emit_norm=emit_norm, final_norm=final_norm),
        out_shape=out_shape,
        grid=(n_i, n_j),
        in_specs=in_specs,
        out_specs=out_specs,
        scratch_shapes=[pltpu.VMEM((2, TM, D_MODEL), F32), pltpu.VMEM((TM, D_MODEL), BF16),
                        pltpu.SemaphoreType.DMA((2,)), pltpu.SemaphoreType.DMA((2,)),
                        pltpu.SemaphoreType.DMA(())],
        compiler_params=_params(("arbitrary", "arbitrary")),
        name="ffn",
    )(*args)
    res = list(res)
    outs = [res.pop(0) for _ in out_rows]
    xn = res.pop(0) if emit_norm else None
    return outs, xn, (res if convert_next else None)


def _proj_kernel(a_ref, w_ref, b_ref, o_ref, wb_ref, *, gelu):
    @pl.when(pl.program_id(1) == 0)
    def _():
        wb_ref[...] = w_ref[...].astype(BF16)

    y = jnp.dot(a_ref[...], wb_ref[...], preferred_element_type=F32)
    if gelu:
        y = y + b_ref[...]
        y = 0.5 * y * (1.0 + lax.erf(y * (2.0 ** -0.5)))
    o_ref[...] = y


def _proj(a, w, b, gelu):
    rows, k = a.shape
    n = w.shape[1]
    return pl.pallas_call(
        functools.partial(_proj_kernel, gelu=gelu),
        out_shape=jax.ShapeDtypeStruct((rows, n), F32),
        grid=(n // TN, rows // TM),
        in_specs=[
            pl.BlockSpec((TM, k), lambda j, i: (i, 0)),
            pl.BlockSpec((k, TN), lambda j, i: (0, j)),
            pl.BlockSpec((1, TN), lambda j, i: (0, j)),
        ],
        out_specs=pl.BlockSpec((TM, TN), lambda j, i: (i, j)),
        scratch_shapes=[pltpu.VMEM((k, TN), BF16)],
        compiler_params=_params(("arbitrary", "arbitrary")),
        name="proj_gelu" if gelu else "proj",
    )(a, w, b)


def _proj_residual_kernel(h_ref, w_ref, x_ref, o_ref, wb_ref):
    @pl.when(pl.program_id(1) == 0)
    def _():
        wb_ref[...] = w_ref[...].astype(BF16)

    o_ref[...] = x_ref[...] + jnp.dot(h_ref[...], wb_ref[...], preferred_element_type=F32)


def _proj_residual(h, w, x):
    rows, k = h.shape
    return pl.pallas_call(
        _proj_residual_kernel,
        out_shape=jax.ShapeDtypeStruct((rows, D_MODEL), F32),
        grid=(D_MODEL // TN, rows // TM),
        in_specs=[
            pl.BlockSpec((TM, k), lambda j, i: (i, 0)),
            pl.BlockSpec((k, TN), lambda j, i: (0, j)),
            pl.BlockSpec((TM, TN), lambda j, i: (i, j)),
        ],
        out_specs=pl.BlockSpec((TM, TN), lambda j, i: (i, j)),
        scratch_shapes=[pltpu.VMEM((k, TN), BF16)],
        compiler_params=_params(("arbitrary", "arbitrary")),
        name="proj_residual",
    )(h, w, x)


def _conv_prompt_kernel(p_ref, wa_ref, ba_ref, lg_ref, lb_ref, wb_ref,
                        h_ref, nsa_ref, nsb_ref, abuf, bbuf, cbuf):
    t = pl.program_id(1)

    @pl.when(t == 0)
    def _():
        abuf[0:A_HALO, :] = jnp.zeros((A_HALO, D_A), F32)
        bbuf[0:B_HALO, :] = jnp.zeros((B_HALO, D_B), F32)

    abuf[A_HALO:A_HALO + CONV_T, :] = p_ref[:, 0:D_A] * jax.nn.sigmoid(p_ref[:, D_A:2 * D_A])
    bbuf[B_HALO:B_HALO + CONV_T, :] = (p_ref[:, 2 * D_A + D_B:2 * D_A + 2 * D_B]
                                       * p_ref[:, 2 * D_A + 2 * D_B:2 * D_A + 3 * D_B])

    a0 = A_HALO - (K_A - 1)
    for c in range(D_A // 128):
        cs = slice(c * 128, (c + 1) * 128)
        for r in range(CONV_T // CONV_RB):
            r0 = r * CONV_RB
            acc = wa_ref[0:1, cs] * abuf[a0 + r0:a0 + r0 + CONV_RB, cs]
            for k in range(1, K_A):
                acc = acc + wa_ref[k:k + 1, cs] * abuf[a0 + k + r0:a0 + k + r0 + CONV_RB, cs]
            cbuf[r0:r0 + CONV_RB, cs] = acc + ba_ref[:, cs]

    a_ln = _layernorm_rows(cbuf[...], lg_ref[...], lb_ref[...])
    h_ref[:, 0:D_A] = (a_ln * jax.nn.sigmoid(a_ln)).astype(BF16)

    b0 = B_HALO - (K_B - 1)
    b_c = wb_ref[0:1, :] * bbuf[b0:b0 + CONV_T, :]
    for k in range(1, K_B):
        b_c = b_c + wb_ref[k:k + 1, :] * bbuf[b0 + k:b0 + k + CONV_T, :]
    h_ref[:, D_A:D_A + D_B] = (p_ref[:, 2 * D_A:2 * D_A + D_B] * b_c).astype(BF16)

    nsa_ref[0] = abuf[CONV_T + a0:CONV_T + A_HALO, :]
    nsb_ref[0] = bbuf[CONV_T + b0:CONV_T + B_HALO, :]
    abuf[0:A_HALO, :] = abuf[CONV_T:CONV_T + A_HALO, :]
    bbuf[0:B_HALO, :] = bbuf[CONV_T:CONV_T + B_HALO, :]


def _conv_prompt(p, wa, ba, lg, lb, wb, batch, seq):
    rows = batch * seq
    nt = seq // CONV_T
    return pl.pallas_call(
        _conv_prompt_kernel,
        out_shape=(jax.ShapeDtypeStruct((rows, D_A + D_B), BF16),
                   jax.ShapeDtypeStruct((batch, K_A - 1, D_A), F32),
                   jax.ShapeDtypeStruct((batch, K_B - 1, D_B), F32)),
        grid=(batch, nt),
        in_specs=[
            pl.BlockSpec((CONV_T, p.shape[1]), lambda b, t: (b * nt + t, 0)),
            pl.BlockSpec((K_A, D_A), lambda b, t: (0, 0)),
            pl.BlockSpec((1, D_A), lambda b, t: (0, 0)),
            pl.BlockSpec((1, D_A), lambda b, t: (0, 0)),
            pl.BlockSpec((1, D_A), lambda b, t: (0, 0)),
            pl.BlockSpec((K_B, D_B), lambda b, t: (0, 0)),
        ],
        out_specs=(pl.BlockSpec((CONV_T, D_A + D_B), lambda b, t: (b * nt + t, 0)),
                   pl.BlockSpec((1, K_A - 1, D_A), lambda b, t: (b, 0, 0)),
                   pl.BlockSpec((1, K_B - 1, D_B), lambda b, t: (b, 0, 0))),
        scratch_shapes=[pltpu.VMEM((A_HALO + CONV_T, D_A), F32),
                        pltpu.VMEM((B_HALO + CONV_T, D_B), F32),
                        pltpu.VMEM((CONV_T, D_A), F32)],
        compiler_params=_params(("arbitrary", "arbitrary")),
        name="conv_prompt",
    )(p, wa, ba, lg, lb, wb)


def _conv_sample_kernel(p_ref, sa_ref, sb_ref, wa_ref, ba_ref, lg_ref, lb_ref, wb_ref,
                        h_ref, anew_ref, bnew_ref):
    a = p_ref[:, 0:D_A] * jax.nn.sigmoid(p_ref[:, D_A:2 * D_A])
    acc = wa_ref[K_A - 1:K_A, :] * a
    for k in range(K_A - 1):
        acc = acc + wa_ref[k:k + 1, :] * sa_ref[:, k, :]
    a_ln = _layernorm_rows(acc + ba_ref[...], lg_ref[...], lb_ref[...])
    h_ref[:, 0:D_A] = (a_ln * jax.nn.sigmoid(a_ln)).astype(BF16)
    anew_ref[...] = a

    ch = (p_ref[:, 2 * D_A + D_B:2 * D_A + 2 * D_B]
          * p_ref[:, 2 * D_A + 2 * D_B:2 * D_A + 3 * D_B])
    b_c = wb_ref[K_B - 1:K_B, :] * ch
    for k in range(K_B - 1):
        b_c = b_c + wb_ref[k:k + 1, :] * sb_ref[:, k, :]
    h_ref[:, D_A:D_A + D_B] = (p_ref[:, 2 * D_A:2 * D_A + D_B] * b_c).astype(BF16)
    bnew_ref[...] = ch


def _conv_sample(p, sa, sb, wa, ba, lg, lb, wb, row0):
    nb = sa.shape[0]
    blk0 = row0 // SAMPLE_BS
    return pl.pallas_call(
        _conv_sample_kernel,
        out_shape=(jax.ShapeDtypeStruct((nb, D_A + D_B), BF16),
                   jax.ShapeDtypeStruct((nb, D_A), F32),
                   jax.ShapeDtypeStruct((nb, D_B), F32)),
        grid=(nb // SAMPLE_BS,),
        in_specs=[
            pl.BlockSpec((SAMPLE_BS, p.shape[1]), lambda s: (blk0 + s, 0)),
            pl.BlockSpec((SAMPLE_BS, K_A - 1, D_A), lambda s: (s, 0, 0)),
            pl.BlockSpec((SAMPLE_BS, K_B - 1, D_B), lambda s: (s, 0, 0)),
            pl.BlockSpec((K_A, D_A), lambda s: (0, 0)),
            pl.BlockSpec((1, D_A), lambda s: (0, 0)),
            pl.BlockSpec((1, D_A), lambda s: (0, 0)),
            pl.BlockSpec((1, D_A), lambda s: (0, 0)),
            pl.BlockSpec((K_B, D_B), lambda s: (0, 0)),
        ],
        out_specs=(pl.BlockSpec((SAMPLE_BS, D_A + D_B), lambda s: (s, 0)),
                   pl.BlockSpec((SAMPLE_BS, D_A), lambda s: (s, 0)),
                   pl.BlockSpec((SAMPLE_BS, D_B), lambda s: (s, 0))),
        compiler_params=_params(("arbitrary",)),
        name="conv_sample",
    )(p, sa, sb, wa, ba, lg, lb, wb)


def _gate_kernel(z_ref, lg_ref, lb_ref, ws_ref, bs_ref, o_ref, v_ref, *, n_prompt_chunks):
    i = pl.program_id(0)
    is_sample = i >= n_prompt_chunks
    vn = _layernorm_rows(z_ref[:, D_C:2 * D_C], lg_ref[...], lb_ref[...])
    vb = vn.astype(BF16)
    row = lax.broadcasted_iota(jnp.int32, (CHUNK, CHUNK), 0)
    col = lax.broadcasted_iota(jnp.int32, (CHUNK, CHUNK), 1)
    keep = (col <= row) & (col >= jnp.where(is_sample, row, 0))
    for h in range(H_C):
        hs = slice(h * DH_C, (h + 1) * DH_C)
        w = jnp.where(keep, ws_ref[0, h], 0.0).astype(BF16)
        s = jnp.dot(w, vb[:, hs], preferred_element_type=F32) + bs_ref[0][:, h:h + 1]
        o_ref[:, hs] = (z_ref[:, hs] * s).astype(BF16)

    @pl.when(i == pl.num_programs(0) - 1)
    def _():
        v_ref[...] = vn


def _gate(z, lg, lb, ws_all, bs_all, n_prompt_chunks):
    rows = z.shape[0]
    sel = lambda i: jnp.where(i >= n_prompt_chunks, 1, 0)
    return pl.pallas_call(
        functools.partial(_gate_kernel, n_prompt_chunks=n_prompt_chunks),
        out_shape=(jax.ShapeDtypeStruct((rows, D_C), BF16),
                   jax.ShapeDtypeStruct((CHUNK, D_C), F32)),
        grid=(rows // CHUNK,),
        in_specs=[
            pl.BlockSpec((CHUNK, 2 * D_C), lambda i: (i, 0)),
            pl.BlockSpec((1, D_C), lambda i: (0, 0)),
            pl.BlockSpec((1, D_C), lambda i: (0, 0)),
            pl.BlockSpec((1, H_C, CHUNK, CHUNK), lambda i: (sel(i), 0, 0, 0)),
            pl.BlockSpec((1, CHUNK, H_C), lambda i: (sel(i), 0, 0)),
        ],
        out_specs=(pl.BlockSpec((CHUNK, D_C), lambda i: (i, 0)),
                   pl.BlockSpec((CHUNK, D_C), lambda i: (0, 0))),
        compiler_params=_params(("arbitrary",)),
        name="gate",
    )(z, lg, lb, ws_all, bs_all)


def kernel(x_prompt, x_sample, state_conv_a, state_conv_b, norm_g, ffn_w1, ffn_w3, ffn_w2,
           ab_w_in, a_conv_w, a_conv_b, a_ln_g, a_ln_b, b_conv_w, ab_w_out,
           c_w_in, c_b_in, c_ln_g, c_ln_b, c_w_s, c_b_s, c_w_out, final_g):
    batch, seq, _ = x_prompt.shape
    nb, dec_seq, _ = x_sample.shape
    depth = norm_g.shape[0]
    n_prompt = batch * seq
    rows = n_prompt + nb
    assert dec_seq == 1 and nb == CHUNK and seq % CONV_T == 0 and n_prompt % CHUNK == 0

    row = lambda v: v.reshape(1, -1)
    ffn_f32 = lambda i, s: (ffn_w1[i, s], ffn_w3[i, s], ffn_w2[i, s])
    w_bf16 = [w.astype(BF16) for w in ffn_f32(0, 0)]

    xs = [x_prompt.reshape(n_prompt, D_MODEL), x_sample.reshape(nb, D_MODEL)]
    new_a_p, new_b_p, new_a_s, new_b_s, new_v_s = [], [], [], [], []
    for i in range(depth):
        j = i // 2
        (x,), xn, w_bf16 = _ffn(xs, row(norm_g[i, 0]), *w_bf16, row(norm_g[i, 1]), (rows,),
                                next_w=ffn_f32(i, 1), emit_norm=True)
        if i % 2 == 0:
            p = _proj(xn, ab_w_in[j], jnp.zeros((1, ab_w_in.shape[2]), F32), False)
            conv_w = (a_conv_w[j], row(a_conv_b[j]), row(a_ln_g[j]), row(a_ln_b[j]), b_conv_w[j])
            h_p, sa_p, sb_p = _conv_prompt(p, *conv_w, batch, seq)
            h_s, a_new, b_new = _conv_sample(p, state_conv_a[j], state_conv_b[j], *conv_w, n_prompt)
            h = jnp.concatenate([h_p, h_s], axis=0)
            new_a_p.append(sa_p)
            new_b_p.append(sb_p)
            new_a_s.append(jnp.concatenate([state_conv_a[j][:, 1:], a_new[:, None, :]], axis=1))
            new_b_s.append(jnp.concatenate([state_conv_b[j][:, 1:], b_new[:, None, :]], axis=1))
            x = _proj_residual(h, ab_w_out[j], x)
        else:
            z = _proj(xn, c_w_in[j], row(c_b_in[j]), True)
            w_s = c_w_s[j][:, :CHUNK, :CHUNK]
            ws_all = jnp.stack([w_s, jnp.broadcast_to(w_s[:, :1, :1], w_s.shape)])
            b_s = c_b_s[j][:, :CHUNK].T
            bs_all = jnp.stack([b_s, jnp.broadcast_to(b_s[:1], b_s.shape)])
            h, v_s = _gate(z, row(c_ln_g[j]), row(c_ln_b[j]), ws_all, bs_all, n_prompt // CHUNK)
            new_v_s.append(v_s.reshape(nb, 1, D_C))
            x = _proj_residual(h, c_w_out[j], x)
        last = i == depth - 1
        xs, _, w_bf16 = _ffn([x], row(norm_g[i, 2]), *w_bf16, row(final_g),
                             (n_prompt, nb) if last else (rows,),
                             next_w=None if last else ffn_f32(i + 1, 0), final_norm=last)

    y_prompt, y_sample = xs
    return (y_prompt.reshape(batch, seq, D_MODEL), y_sample.reshape(nb, 1, D_MODEL),
            jnp.stack(new_a_p), jnp.stack(new_b_p), jnp.stack(new_a_s), jnp.stack(new_b_s),
            jnp.stack(new_v_s))
```

```python
import functools

import jax
import jax.numpy as jnp
from jax import lax
from jax.experimental import pallas as pl
from jax.experimental.pallas import tpu as pltpu

F32 = jnp.float32
BF16 = jnp.bfloat16

D_MODEL = 2048
D_FF = 5632
D_A = 1024
D_B = 1024
K_A = 31
K_B = 3
D_C = 2048
H_C = 8
DH_C = D_C // H_C
CHUNK = 128
EPS = 1e-6

VMEM_LIMIT_BYTES = 56 * 1024 * 1024

TM = 1040
NORM_ROWS = 208
TF = 512
TN = 1024
CONV_T = 256
CONV_RB = 64
A_HALO = 32
B_HALO = 8
SUBLANES = 8
A_SHIFT_ROWS = A_HALO + CONV_T - SUBLANES
SAMPLE_BS = 32
GATE_CHUNKS = 5


def _params(sem):
    return pltpu.CompilerParams(dimension_semantics=sem, vmem_limit_bytes=VMEM_LIMIT_BYTES)


def _rms_rows(x, g):
    ms = jnp.mean(x * x, axis=-1, keepdims=True)
    return x * lax.rsqrt(ms + EPS) * g


def _layernorm_rows(x, g, b):
    mu = jnp.mean(x, axis=-1, keepdims=True)
    xc = x - mu
    var = jnp.mean(xc * xc, axis=-1, keepdims=True)
    return xc * lax.rsqrt(var + EPS) * g + b


def _norm_chunks(src_ref, g_ref, dst_ref, dtype):
    def body(c, carry):
        r = pl.ds(pl.multiple_of(c * NORM_ROWS, NORM_ROWS), NORM_ROWS)
        dst_ref[r, :] = _rms_rows(src_ref[r, :], g_ref[...]).astype(dtype)
        return carry
    lax.fori_loop(0, TM // NORM_ROWS, body, 0)


def _tile_pieces(tile, seg_rows):
    lo, hi = tile * TM, (tile + 1) * TM
    pieces, base = [], 0
    for k, n in enumerate(seg_rows):
        a, b = max(lo, base), min(hi, base + n)
        if a < b:
            pieces.append((k, a - base, a - lo, b - a))
        base += n
    return pieces


def _ffn_kernel(*refs, in_rows, out_rows, convert_next, emit_norm, final_norm):
    n_in, n_out = len(in_rows), len(out_rows)
    it = iter(refs)
    x_srcs = [next(it) for _ in range(n_in)]
    g_ref, w1_ref, w3_ref, w2_ref, post_g_ref = (next(it) for _ in range(5))
    next_w = [next(it) for _ in range(3)] if convert_next else []
    o_dsts = [next(it) for _ in range(n_out)]
    xn_out = next(it) if emit_norm else None
    next_w_out = [next(it) for _ in range(3)] if convert_next else []
    acc_ref, xn_ref, load_sem, store_sem = (next(it) for _ in range(4))
    xn_next_ref, xn_sem = (next(it), next(it)) if emit_norm else (None, None)

    i = pl.program_id(0)
    j = pl.program_id(1)
    n_i = pl.num_programs(0)
    n_j = pl.num_programs(1)
    n_tiles = sum(in_rows) // TM

    def load_copies(t):
        return [pltpu.make_async_copy(x_srcs[k].at[pl.ds(r0, n)],
                                      acc_ref.at[t % 2, pl.ds(d0, n)], load_sem.at[t % 2])
                for k, r0, d0, n in _tile_pieces(t, in_rows)]

    def store_copies(t):
        return [pltpu.make_async_copy(acc_ref.at[t % 2, pl.ds(d0, n)],
                                      o_dsts[k].at[pl.ds(r0, n)], store_sem.at[t % 2])
                for k, r0, d0, n in _tile_pieces(t, out_rows)]

    def xn_copy(t):
        return pltpu.make_async_copy(xn_next_ref, xn_out.at[pl.ds(t * TM, TM)], xn_sem)

    def on_tile(tile, fn):
        for t in range(n_tiles):
            @pl.when(tile == t)
            def _(t=t):
                fn(t)

    acc = acc_ref.at[lax.rem(i, 2)]

    @pl.when(j == 0)
    def _():
        @pl.when(i == 0)
        def _():
            for cp in load_copies(0):
                cp.start()
        on_tile(i, lambda t: [cp.wait() for cp in load_copies(t)])
        _norm_chunks(acc, g_ref, xn_ref, BF16)

    @pl.when(j == 2)
    def _():
        @pl.when(i > 0)
        def _():
            on_tile(i - 1, lambda t: [cp.wait() for cp in store_copies(t)])
            if emit_norm:
                on_tile(i - 1, lambda t: xn_copy(t).wait())

        @pl.when(i + 1 < n_i)
        def _():
            on_tile(i + 1, lambda t: [cp.start() for cp in load_copies(t)])

    xn = xn_ref[...]
    h1 = jnp.dot(xn, w1_ref[...], preferred_element_type=F32)
    h3 = jnp.dot(xn, w3_ref[...], preferred_element_type=F32)
    h = (0.5 * (h1 * jax.nn.sigmoid(h1)) * h3).astype(BF16)
    for n in range(D_MODEL // TF):
        cs = slice(n * TF, (n + 1) * TF)
        acc[:, cs] += jnp.dot(h, w2_ref[:, cs], preferred_element_type=F32)

    if convert_next:
        for src, dst in zip(next_w, next_w_out):
            dst[...] = src[...].astype(BF16)

    @pl.when(j == n_j - 1)
    def _():
        if final_norm:
            _norm_chunks(acc, post_g_ref, acc, F32)
        on_tile(i, lambda t: [cp.start() for cp in store_copies(t)])
        if emit_norm:
            _norm_chunks(acc, post_g_ref, xn_next_ref, BF16)
            on_tile(i, lambda t: xn_copy(t).start())

        @pl.when(i == n_i - 1)
        def _():
            on_tile(i, lambda t: [cp.wait() for cp in store_copies(t)])
            if emit_norm:
                on_tile(i, lambda t: xn_copy(t).wait())


def _ffn(xs, g, w1, w3, w2, post_g, out_rows, next_w=None, emit_norm=False, final_norm=False):
    in_rows = tuple(a.shape[0] for a in xs)
    rows = sum(in_rows)
    assert rows == sum(out_rows) and rows % TM == 0 and D_FF % TF == 0
    n_i, n_j = rows // TM, D_FF // TF
    assert n_j > 2 and D_MODEL % n_i == 0
    convert_next = next_w is not None
    any_spec = pl.BlockSpec(memory_space=pl.ANY)
    row_spec = pl.BlockSpec((1, D_MODEL), lambda i, j: (0, 0))
    up_piece = pl.BlockSpec((D_MODEL // n_i, TF), lambda i, j: (i, j))
    down_piece = pl.BlockSpec((TF, D_MODEL // n_i), lambda i, j: (j, i))

    in_specs = [any_spec] * len(xs) + [
        row_spec,
        pl.BlockSpec((D_MODEL, TF), lambda i, j: (0, j)),
        pl.BlockSpec((D_MODEL, TF), lambda i, j: (0, j)),
        pl.BlockSpec((TF, D_MODEL), lambda i, j: (j, 0)),
        row_spec,
    ]
    args = list(xs) + [g, w1, w3, w2, post_g]
    out_shape = [jax.ShapeDtypeStruct((n, D_MODEL), F32) for n in out_rows]
    out_specs = [any_spec] * len(out_rows)
    if emit_norm:
        out_shape.append(jax.ShapeDtypeStruct((rows, D_MODEL), BF16))
        out_specs.append(any_spec)
    if convert_next:
        stacked, (li, si) = next_w
        in_specs += [
            pl.BlockSpec((None, None, D_MODEL // n_i, TF), lambda i, j: (li, si, i, j)),
            pl.BlockSpec((None, None, D_MODEL // n_i, TF), lambda i, j: (li, si, i, j)),
            pl.BlockSpec((None, None, TF, D_MODEL // n_i), lambda i, j: (li, si, j, i)),
        ]
        args += list(stacked)
        out_shape += [jax.ShapeDtypeStruct(w.shape[2:], BF16) for w in stacked]
        out_specs += [up_piece, up_piece, down_piece]
    scratch = [pltpu.VMEM((2, TM, D_MODEL), F32), pltpu.VMEM((TM, D_MODEL), BF16),
               pltpu.SemaphoreType.DMA((2,)), pltpu.SemaphoreType.DMA((2,))]
    if emit_norm:
        scratch += [pltpu.VMEM((TM, D_MODEL), BF16), pltpu.SemaphoreType.DMA(())]

    res = pl.pallas_call(
        functools.partial(_ffn_kernel, in_rows=in_rows, out_rows=tuple(out_rows),
                          convert_next=convert_next, emit_norm=emit_norm, final_norm=final_norm),
        out_shape=out_shape,
        grid=(n_i, n_j),
        in_specs=in_specs,
        out_specs=out_specs,
        scratch_shapes=scratch,
        compiler_params=_params(("arbitrary", "arbitrary")),
        name="ffn",
    )(*args)
    res = list(res)
    outs = [res.pop(0) for _ in out_rows]
    xn = res.pop(0) if emit_norm else None
    return outs, xn, (res if convert_next else None)


def _proj_kernel(a_ref, w_ref, b_ref, o_ref, wb_ref, *, gelu):
    @pl.when(pl.program_id(1) == 0)
    def _():
        wb_ref[...] = w_ref[...].astype(BF16)

    y = jnp.dot(a_ref[...], wb_ref[...], preferred_element_type=F32)
    if gelu:
        y = y + b_ref[...]
        y = 0.5 * y * (1.0 + lax.erf(y * (2.0 ** -0.5)))
    o_ref[...] = y


def _proj(a, w, b, gelu):
    rows, k = a.shape
    n = w.shape[1]
    return pl.pallas_call(
        functools.partial(_proj_kernel, gelu=gelu),
        out_shape=jax.ShapeDtypeStruct((rows, n), F32),
        grid=(n // TN, rows // TM),
        in_specs=[
            pl.BlockSpec((TM, k), lambda j, i: (i, 0)),
            pl.BlockSpec((k, TN), lambda j, i: (0, j)),
            pl.BlockSpec((1, TN), lambda j, i: (0, j)),
        ],
        out_specs=pl.BlockSpec((TM, TN), lambda j, i: (i, j)),
        scratch_shapes=[pltpu.VMEM((k, TN), BF16)],
        compiler_params=_params(("arbitrary", "arbitrary")),
        name="proj_gelu" if gelu else "proj",
    )(a, w, b)


def _proj_residual_kernel(h_ref, w_ref, x_ref, o_ref, wb_ref):
    @pl.when(pl.program_id(1) == 0)
    def _():
        wb_ref[...] = w_ref[...].astype(BF16)

    o_ref[...] = x_ref[...] + jnp.dot(h_ref[...], wb_ref[...], preferred_element_type=F32)


def _proj_residual(h, w, x):
    rows, k = h.shape
    return pl.pallas_call(
        _proj_residual_kernel,
        out_shape=jax.ShapeDtypeStruct((rows, D_MODEL), F32),
        grid=(D_MODEL // TN, rows // TM),
        in_specs=[
            pl.BlockSpec((TM, k), lambda j, i: (i, 0)),
            pl.BlockSpec((k, TN), lambda j, i: (0, j)),
            pl.BlockSpec((TM, TN), lambda j, i: (i, j)),
        ],
        out_specs=pl.BlockSpec((TM, TN), lambda j, i: (i, j)),
        scratch_shapes=[pltpu.VMEM((k, TN), BF16)],
        compiler_params=_params(("arbitrary", "arbitrary")),
        name="proj_residual",
    )(h, w, x)


def _conv_prompt_kernel(p_ref, wa_ref, ba_ref, lg_ref, lb_ref, wb_ref,
                        h_ref, nsa_ref, nsb_ref, abuf, bbuf, cbuf, sbuf):
    t = pl.program_id(1)

    @pl.when(t == 0)
    def _():
        abuf[0:A_HALO, :] = jnp.zeros((A_HALO, D_A), F32)
        bbuf[0:B_HALO, :] = jnp.zeros((B_HALO, D_B), F32)

    abuf[A_HALO:A_HALO + CONV_T, :] = p_ref[:, 0:D_A] * jax.nn.sigmoid(p_ref[:, D_A:2 * D_A])
    bbuf[B_HALO:B_HALO + CONV_T, :] = (p_ref[:, 2 * D_A + D_B:2 * D_A + 2 * D_B]
                                       * p_ref[:, 2 * D_A + 2 * D_B:2 * D_A + 3 * D_B])

    for s in range(1, SUBLANES):
        sbuf[s - 1] = abuf[s:s + A_SHIFT_ROWS, :]

    a0 = A_HALO - (K_A - 1)

    def tap(k, r0, cs):
        q, s = divmod(a0 + k, SUBLANES)
        src = abuf if s == 0 else sbuf.at[s - 1]
        return wa_ref[k:k + 1, cs] * src[SUBLANES * q + r0:SUBLANES * q + r0 + CONV_RB, cs]

    for c in range(D_A // 128):
        cs = slice(c * 128, (c + 1) * 128)
        for r in range(CONV_T // CONV_RB):
            r0 = r * CONV_RB
            acc = tap(0, r0, cs)
            for k in range(1, K_A):
                acc = acc + tap(k, r0, cs)
            cbuf[r0:r0 + CONV_RB, cs] = acc + ba_ref[:, cs]

    a_ln = _layernorm_rows(cbuf[...], lg_ref[...], lb_ref[...])
    h_ref[:, 0:D_A] = (a_ln * jax.nn.sigmoid(a_ln)).astype(BF16)

    b0 = B_HALO - (K_B - 1)
    b_c = wb_ref[0:1, :] * bbuf[b0:b0 + CONV_T, :]
    for k in range(1, K_B):
        b_c = b_c + wb_ref[k:k + 1, :] * bbuf[b0 + k:b0 + k + CONV_T, :]
    h_ref[:, D_A:D_A + D_B] = (p_ref[:, 2 * D_A:2 * D_A + D_B] * b_c).astype(BF16)

    nsa_ref[0] = abuf[CONV_T + a0:CONV_T + A_HALO, :]
    nsb_ref[0] = bbuf[CONV_T + b0:CONV_T + B_HALO, :]
    abuf[0:A_HALO, :] = abuf[CONV_T:CONV_T + A_HALO, :]
    bbuf[0:B_HALO, :] = bbuf[CONV_T:CONV_T + B_HALO, :]


def _conv_prompt(p, wa, ba, lg, lb, wb, batch, seq):
    rows = batch * seq
    nt = seq // CONV_T
    return pl.pallas_call(
        _conv_prompt_kernel,
        out_shape=(jax.ShapeDtypeStruct((rows, D_A + D_B), BF16),
                   jax.ShapeDtypeStruct((batch, K_A - 1, D_A), F32),
                   jax.ShapeDtypeStruct((batch, K_B - 1, D_B), F32)),
        grid=(batch, nt),
        in_specs=[
            pl.BlockSpec((CONV_T, p.shape[1]), lambda b, t: (b * nt + t, 0)),
            pl.BlockSpec((K_A, D_A), lambda b, t: (0, 0)),
            pl.BlockSpec((1, D_A), lambda b, t: (0, 0)),
            pl.BlockSpec((1, D_A), lambda b, t: (0, 0)),
            pl.BlockSpec((1, D_A), lambda b, t: (0, 0)),
            pl.BlockSpec((K_B, D_B), lambda b, t: (0, 0)),
        ],
        out_specs=(pl.BlockSpec((CONV_T, D_A + D_B), lambda b, t: (b * nt + t, 0)),
                   pl.BlockSpec((1, K_A - 1, D_A), lambda b, t: (b, 0, 0)),
                   pl.BlockSpec((1, K_B - 1, D_B), lambda b, t: (b, 0, 0))),
        scratch_shapes=[pltpu.VMEM((A_HALO + CONV_T, D_A), F32),
                        pltpu.VMEM((B_HALO + CONV_T, D_B), F32),
                        pltpu.VMEM((CONV_T, D_A), F32),
                        pltpu.VMEM((SUBLANES - 1, A_SHIFT_ROWS, D_A), F32)],
        compiler_params=_params(("arbitrary", "arbitrary")),
        name="conv_prompt",
    )(p, wa, ba, lg, lb, wb)


def _conv_sample_kernel(p_ref, sa_ref, sb_ref, wa_ref, ba_ref, lg_ref, lb_ref, wb_ref,
                        h_ref, anew_ref, bnew_ref):
    a = p_ref[:, 0:D_A] * jax.nn.sigmoid(p_ref[:, D_A:2 * D_A])
    acc = wa_ref[K_A - 1:K_A, :] * a
    for k in range(K_A - 1):
        acc = acc + wa_ref[k:k + 1, :] * sa_ref[:, k, :]
    a_ln = _layernorm_rows(acc + ba_ref[...], lg_ref[...], lb_ref[...])
    h_ref[:, 0:D_A] = (a_ln * jax.nn.sigmoid(a_ln)).astype(BF16)
    anew_ref[...] = a

    ch = (p_ref[:, 2 * D_A + D_B:2 * D_A + 2 * D_B]
          * p_ref[:, 2 * D_A + 2 * D_B:2 * D_A + 3 * D_B])
    b_c = wb_ref[K_B - 1:K_B, :] * ch
    for k in range(K_B - 1):
        b_c = b_c + wb_ref[k:k + 1, :] * sb_ref[:, k, :]
    h_ref[:, D_A:D_A + D_B] = (p_ref[:, 2 * D_A:2 * D_A + D_B] * b_c).astype(BF16)
    bnew_ref[...] = ch


def _conv_sample(p, sa, sb, wa, ba, lg, lb, wb, row0):
    nb = sa.shape[0]
    blk0 = row0 // SAMPLE_BS
    return pl.pallas_call(
        _conv_sample_kernel,
        out_shape=(jax.ShapeDtypeStruct((nb, D_A + D_B), BF16),
                   jax.ShapeDtypeStruct((nb, D_A), F32),
                   jax.ShapeDtypeStruct((nb, D_B), F32)),
        grid=(nb // SAMPLE_BS,),
        in_specs=[
            pl.BlockSpec((SAMPLE_BS, p.shape[1]), lambda s: (blk0 + s, 0)),
            pl.BlockSpec((SAMPLE_BS, K_A - 1, D_A), lambda s: (s, 0, 0)),
            pl.BlockSpec((SAMPLE_BS, K_B - 1, D_B), lambda s: (s, 0, 0)),
            pl.BlockSpec((K_A, D_A), lambda s: (0, 0)),
            pl.BlockSpec((1, D_A), lambda s: (0, 0)),
            pl.BlockSpec((1, D_A), lambda s: (0, 0)),
            pl.BlockSpec((1, D_A), lambda s: (0, 0)),
            pl.BlockSpec((K_B, D_B), lambda s: (0, 0)),
        ],
        out_specs=(pl.BlockSpec((SAMPLE_BS, D_A + D_B), lambda s: (s, 0)),
                   pl.BlockSpec((SAMPLE_BS, D_A), lambda s: (s, 0)),
                   pl.BlockSpec((SAMPLE_BS, D_B), lambda s: (s, 0))),
        compiler_params=_params(("arbitrary",)),
        name="conv_sample",
    )(p, sa, sb, wa, ba, lg, lb, wb)


def _gate_kernel(z_ref, lg_ref, lb_ref, ws_ref, bs_ref, o_ref, v_ref, *, n_prompt_chunks):
    i = pl.program_id(0)
    row = lax.broadcasted_iota(jnp.int32, (CHUNK, CHUNK), 0)
    col = lax.broadcasted_iota(jnp.int32, (CHUNK, CHUNK), 1)
    for c in range(GATE_CHUNKS):
        rs = slice(c * CHUNK, (c + 1) * CHUNK)
        is_sample = i * GATE_CHUNKS + c >= n_prompt_chunks
        sel = jnp.where(is_sample, 1, 0)
        vn = _layernorm_rows(z_ref[rs, D_C:2 * D_C], lg_ref[...], lb_ref[...])
        vb = vn.astype(BF16)
        keep = (col <= row) & (col >= jnp.where(is_sample, row, 0))
        for h in range(H_C):
            hs = slice(h * DH_C, (h + 1) * DH_C)
            w = jnp.where(keep, ws_ref[sel, h], 0.0).astype(BF16)
            s = jnp.dot(w, vb[:, hs], preferred_element_type=F32) + bs_ref[sel][:, h:h + 1]
            o_ref[rs, hs] = (z_ref[rs, hs] * s).astype(BF16)

        if c == GATE_CHUNKS - 1:
            @pl.when(i == pl.num_programs(0) - 1)
            def _():
                v_ref[...] = vn


def _gate(z, lg, lb, ws_all, bs_all, n_prompt_chunks):
    rows = z.shape[0]
    tile = GATE_CHUNKS * CHUNK
    assert rows % tile == 0 and rows // CHUNK == n_prompt_chunks + 1
    return pl.pallas_call(
        functools.partial(_gate_kernel, n_prompt_chunks=n_prompt_chunks),
        out_shape=(jax.ShapeDtypeStruct((rows, D_C), BF16),
                   jax.ShapeDtypeStruct((CHUNK, D_C), F32)),
        grid=(rows // tile,),
        in_specs=[
            pl.BlockSpec((tile, 2 * D_C), lambda i: (i, 0)),
            pl.BlockSpec((1, D_C), lambda i: (0, 0)),
            pl.BlockSpec((1, D_C), lambda i: (0, 0)),
            pl.BlockSpec((2, H_C, CHUNK, CHUNK), lambda i: (0, 0, 0, 0)),
            pl.BlockSpec((2, CHUNK, H_C), lambda i: (0, 0, 0)),
        ],
        out_specs=(pl.BlockSpec((tile, D_C), lambda i: (i, 0)),
                   pl.BlockSpec((CHUNK, D_C), lambda i: (0, 0))),
        compiler_params=_params(("arbitrary",)),
        name="gate",
    )(z, lg, lb, ws_all, bs_all)


def kernel(x_prompt, x_sample, state_conv_a, state_conv_b, norm_g, ffn_w1, ffn_w3, ffn_w2,
           ab_w_in, a_conv_w, a_conv_b, a_ln_g, a_ln_b, b_conv_w, ab_w_out,
           c_w_in, c_b_in, c_ln_g, c_ln_b, c_w_s, c_b_s, c_w_out, final_g):
    batch, seq, _ = x_prompt.shape
    nb, dec_seq, _ = x_sample.shape
    depth = norm_g.shape[0]
    n_prompt = batch * seq
    rows = n_prompt + nb
    assert dec_seq == 1 and nb == CHUNK and seq % CONV_T == 0 and n_prompt % CHUNK == 0

    row = lambda v: v.reshape(1, -1)
    ffn_stacked = (ffn_w1, ffn_w3, ffn_w2)
    w_bf16 = [w[0, 0].astype(BF16) for w in ffn_stacked]

    xs = [x_prompt.reshape(n_prompt, D_MODEL), x_sample.reshape(nb, D_MODEL)]
    new_a_p, new_b_p, new_a_s, new_b_s, new_v_s = [], [], [], [], []
    for i in range(depth):
        j = i // 2
        (x,), xn, w_bf16 = _ffn(xs, row(norm_g[i, 0]), *w_bf16, row(norm_g[i, 1]), (rows,),
                                next_w=(ffn_stacked, (i, 1)), emit_norm=True)
        if i % 2 == 0:
            p = _proj(xn, ab_w_in[j], jnp.zeros((1, ab_w_in.shape[2]), F32), False)
            conv_w = (a_conv_w[j], row(a_conv_b[j]), row(a_ln_g[j]), row(a_ln_b[j]), b_conv_w[j])
            h_p, sa_p, sb_p = _conv_prompt(p, *conv_w, batch, seq)
            h_s, a_new, b_new = _conv_sample(p, state_conv_a[j], state_conv_b[j], *conv_w, n_prompt)
            h = jnp.concatenate([h_p, h_s], axis=0)
            new_a_p.append(sa_p)
            new_b_p.append(sb_p)
            new_a_s.append(jnp.concatenate([state_conv_a[j][:, 1:], a_new[:, None, :]], axis=1))
            new_b_s.append(jnp.concatenate([state_conv_b[j][:, 1:], b_new[:, None, :]], axis=1))
            x = _proj_residual(h, ab_w_out[j], x)
        else:
            z = _proj(xn, c_w_in[j], row(c_b_in[j]), True)
            w_s = c_w_s[j][:, :CHUNK, :CHUNK]
            ws_all = jnp.stack([w_s, jnp.broadcast_to(w_s[:, :1, :1], w_s.shape)])
            b_s = c_b_s[j][:, :CHUNK].T
            bs_all = jnp.stack([b_s, jnp.broadcast_to(b_s[:1], b_s.shape)])
            h, v_s = _gate(z, row(c_ln_g[j]), row(c_ln_b[j]), ws_all, bs_all, n_prompt // CHUNK)
            new_v_s.append(v_s.reshape(nb, 1, D_C))
            x = _proj_residual(h, c_w_out[j], x)
        last = i == depth - 1
        xs, _, w_bf16 = _ffn([x], row(norm_g[i, 2]), *w_bf16, row(final_g),
                             (n_prompt, nb) if last else (rows,),
                             next_w=None if last else (ffn_stacked, (i + 1, 0)), final_norm=last)

    y_prompt, y_sample = xs
    return (y_prompt.reshape(batch, seq, D_MODEL), y_sample.reshape(nb, 1, D_MODEL),
            jnp.stack(new_a_p), jnp.stack(new_b_p), jnp.stack(new_a_s), jnp.stack(new_b_s),
            jnp.stack(new_v_s))
```

```python
import functools

import jax
import jax.numpy as jnp
from jax import lax
from jax.experimental import pallas as pl
from jax.experimental.pallas import tpu as pltpu

F32 = jnp.float32
BF16 = jnp.bfloat16

D_MODEL = 2048
D_FF = 5632
D_A = 1024
D_B = 1024
K_A = 31
K_B = 3
D_C = 2048
H_C = 8
DH_C = D_C // H_C
CHUNK = 128
EPS = 1e-6

VMEM_LIMIT_BYTES = 56 * 1024 * 1024
LANES = 128
SUBLANES = 8

TM = 1040
NORM_ROWS = 208
TF = 512
TN = 1024
W_STAGE = 256
MIX_T = 1024
MIX_C = 256
SAMPLE_BS = 32
CONV_RB = 64
A_HALO = 32
B_HALO = 8
A_SHIFT_ROWS = A_HALO + MIX_T - SUBLANES
GATE_CHUNKS = 5


def _params(sem):
    return pltpu.CompilerParams(dimension_semantics=sem, vmem_limit_bytes=VMEM_LIMIT_BYTES)


def _rms_rows(x, g):
    ms = jnp.mean(x * x, axis=-1, keepdims=True)
    return x * lax.rsqrt(ms + EPS) * g


def _layernorm_rows(x, g, b):
    mu = jnp.mean(x, axis=-1, keepdims=True)
    xc = x - mu
    var = jnp.mean(xc * xc, axis=-1, keepdims=True)
    return xc * lax.rsqrt(var + EPS) * g + b


def _norm_chunks(src_ref, g_ref, dst_ref, dtype):
    def body(c, carry):
        r = pl.ds(pl.multiple_of(c * NORM_ROWS, NORM_ROWS), NORM_ROWS)
        dst_ref[r, :] = _rms_rows(src_ref[r, :], g_ref[...]).astype(dtype)
        return carry
    lax.fori_loop(0, TM // NORM_ROWS, body, 0)


def _tile_pieces(tile, seg_rows):
    lo, hi = tile * TM, (tile + 1) * TM
    pieces, base = [], 0
    for k, n in enumerate(seg_rows):
        a, b = max(lo, base), min(hi, base + n)
        if a < b:
            pieces.append((k, a - base, a - lo, b - a))
        base += n
    return pieces


def _ffn_kernel(*refs, in_rows, out_rows, convert_next, emit_norm, final_norm):
    n_in, n_out = len(in_rows), len(out_rows)
    it = iter(refs)
    x_srcs = [next(it) for _ in range(n_in)]
    g_ref, w1_ref, w3_ref, w2_ref, post_g_ref = (next(it) for _ in range(5))
    next_w = [next(it) for _ in range(3)] if convert_next else []
    o_dsts = [next(it) for _ in range(n_out)]
    xn_out = next(it) if emit_norm else None
    next_w_out = [next(it) for _ in range(3)] if convert_next else []
    acc_ref, xn_ref, load_sem, store_sem = (next(it) for _ in range(4))
    xn_next_ref, xn_sem = (next(it), next(it)) if emit_norm else (None, None)

    i = pl.program_id(0)
    j = pl.program_id(1)
    n_i = pl.num_programs(0)
    n_j = pl.num_programs(1)
    n_tiles = sum(in_rows) // TM

    def load_copies(t):
        return [pltpu.make_async_copy(x_srcs[k].at[pl.ds(r0, n)],
                                      acc_ref.at[t % 2, pl.ds(d0, n)], load_sem.at[t % 2])
                for k, r0, d0, n in _tile_pieces(t, in_rows)]

    def store_copies(t):
        return [pltpu.make_async_copy(acc_ref.at[t % 2, pl.ds(d0, n)],
                                      o_dsts[k].at[pl.ds(r0, n)], store_sem.at[t % 2])
                for k, r0, d0, n in _tile_pieces(t, out_rows)]

    def xn_copy(t):
        return pltpu.make_async_copy(xn_next_ref, xn_out.at[pl.ds(t * TM, TM)], xn_sem)

    def on_tile(tile, fn):
        for t in range(n_tiles):
            @pl.when(tile == t)
            def _(t=t):
                fn(t)

    acc = acc_ref.at[lax.rem(i, 2)]

    @pl.when(j == 0)
    def _():
        @pl.when(i == 0)
        def _():
            for cp in load_copies(0):
                cp.start()
        on_tile(i, lambda t: [cp.wait() for cp in load_copies(t)])
        _norm_chunks(acc, g_ref, xn_ref, BF16)

    @pl.when(j == 2)
    def _():
        @pl.when(i > 0)
        def _():
            on_tile(i - 1, lambda t: [cp.wait() for cp in store_copies(t)])
            if emit_norm:
                on_tile(i - 1, lambda t: xn_copy(t).wait())

        @pl.when(i + 1 < n_i)
        def _():
            on_tile(i + 1, lambda t: [cp.start() for cp in load_copies(t)])

    xn = xn_ref[...]
    h1 = jnp.dot(xn, w1_ref[...], preferred_element_type=F32)
    h3 = jnp.dot(xn, w3_ref[...], preferred_element_type=F32)
    h = (0.5 * (h1 * jax.nn.sigmoid(h1)) * h3).astype(BF16)
    for n in range(D_MODEL // TF):
        cs = slice(n * TF, (n + 1) * TF)
        acc[:, cs] += jnp.dot(h, w2_ref[:, cs], preferred_element_type=F32)

    if convert_next:
        for src, dst in zip(next_w, next_w_out):
            dst[...] = src[...].astype(BF16)

    @pl.when(j == n_j - 1)
    def _():
        if final_norm:
            _norm_chunks(acc, post_g_ref, acc, F32)
        on_tile(i, lambda t: [cp.start() for cp in store_copies(t)])
        if emit_norm:
            _norm_chunks(acc, post_g_ref, xn_next_ref, BF16)
            on_tile(i, lambda t: xn_copy(t).start())

        @pl.when(i == n_i - 1)
        def _():
            on_tile(i, lambda t: [cp.wait() for cp in store_copies(t)])
            if emit_norm:
                on_tile(i, lambda t: xn_copy(t).wait())


def _ffn(xs, g, w1, w3, w2, post_g, out_rows, next_w=None, emit_norm=False, final_norm=False):
    in_rows = tuple(a.shape[0] for a in xs)
    rows = sum(in_rows)
    assert rows == sum(out_rows) and rows % TM == 0 and D_FF % TF == 0
    n_i, n_j = rows // TM, D_FF // TF
    assert n_j > 2 and D_MODEL % n_i == 0
    convert_next = next_w is not None
    any_spec = pl.BlockSpec(memory_space=pl.ANY)
    row_spec = pl.BlockSpec((1, D_MODEL), lambda i, j: (0, 0))
    up_piece = pl.BlockSpec((D_MODEL // n_i, TF), lambda i, j: (i, j))
    down_piece = pl.BlockSpec((TF, D_MODEL // n_i), lambda i, j: (j, i))

    in_specs = [any_spec] * len(xs) + [
        row_spec,
        pl.BlockSpec((D_MODEL, TF), lambda i, j: (0, j)),
        pl.BlockSpec((D_MODEL, TF), lambda i, j: (0, j)),
        pl.BlockSpec((TF, D_MODEL), lambda i, j: (j, 0)),
        row_spec,
    ]
    args = list(xs) + [g, w1, w3, w2, post_g]
    out_shape = [jax.ShapeDtypeStruct((n, D_MODEL), F32) for n in out_rows]
    out_specs = [any_spec] * len(out_rows)
    if emit_norm:
        out_shape.append(jax.ShapeDtypeStruct((rows, D_MODEL), BF16))
        out_specs.append(any_spec)
    if convert_next:
        stacked, (li, si) = next_w
        in_specs += [
            pl.BlockSpec((None, None, D_MODEL // n_i, TF), lambda i, j: (li, si, i, j)),
            pl.BlockSpec((None, None, D_MODEL // n_i, TF), lambda i, j: (li, si, i, j)),
            pl.BlockSpec((None, None, TF, D_MODEL // n_i), lambda i, j: (li, si, j, i)),
        ]
        args += list(stacked)
        out_shape += [jax.ShapeDtypeStruct(w.shape[2:], BF16) for w in stacked]
        out_specs += [up_piece, up_piece, down_piece]
    scratch = [pltpu.VMEM((2, TM, D_MODEL), F32), pltpu.VMEM((TM, D_MODEL), BF16),
               pltpu.SemaphoreType.DMA((2,)), pltpu.SemaphoreType.DMA((2,))]
    if emit_norm:
        scratch += [pltpu.VMEM((TM, D_MODEL), BF16), pltpu.SemaphoreType.DMA(())]

    res = pl.pallas_call(
        functools.partial(_ffn_kernel, in_rows=in_rows, out_rows=tuple(out_rows),
                          convert_next=convert_next, emit_norm=emit_norm, final_norm=final_norm),
        out_shape=out_shape,
        grid=(n_i, n_j),
        in_specs=in_specs,
        out_specs=out_specs,
        scratch_shapes=scratch,
        compiler_params=_params(("arbitrary", "arbitrary")),
        name="ffn",
    )(*args)
    res = list(res)
    outs = [res.pop(0) for _ in out_rows]
    xn = res.pop(0) if emit_norm else None
    return outs, xn, (res if convert_next else None)


def _proj_kernel(a_ref, w_ref, b_ref, o_ref, wb_ref, *, gelu):
    @pl.when(pl.program_id(1) == 0)
    def _():
        wb_ref[...] = w_ref[...].astype(BF16)

    y = jnp.dot(a_ref[...], wb_ref[...], preferred_element_type=F32)
    if gelu:
        y = y + b_ref[...]
        y = 0.5 * y * (1.0 + lax.erf(y * (2.0 ** -0.5)))
    o_ref[...] = y


def _proj(a, w, b, gelu):
    rows, k = a.shape
    n = w.shape[1]
    return pl.pallas_call(
        functools.partial(_proj_kernel, gelu=gelu),
        out_shape=jax.ShapeDtypeStruct((rows, n), F32),
        grid=(n // TN, rows // TM),
        in_specs=[
            pl.BlockSpec((TM, k), lambda j, i: (i, 0)),
            pl.BlockSpec((k, TN), lambda j, i: (0, j)),
            pl.BlockSpec((1, TN), lambda j, i: (0, j)),
        ],
        out_specs=pl.BlockSpec((TM, TN), lambda j, i: (i, j)),
        scratch_shapes=[pltpu.VMEM((k, TN), BF16)],
        compiler_params=_params(("arbitrary", "arbitrary")),
        name="proj_gelu" if gelu else "proj",
    )(a, w, b)


def _proj_residual_kernel(h_ref, w_ref, x_ref, o_ref, wb_ref):
    @pl.when(pl.program_id(1) == 0)
    def _():
        wb_ref[...] = w_ref[...].astype(BF16)

    o_ref[...] = x_ref[...] + jnp.dot(h_ref[...], wb_ref[...], preferred_element_type=F32)


def _proj_residual(h, w, x):
    rows, k = h.shape
    return pl.pallas_call(
        _proj_residual_kernel,
        out_shape=jax.ShapeDtypeStruct((rows, D_MODEL), F32),
        grid=(D_MODEL // TN, rows // TM),
        in_specs=[
            pl.BlockSpec((TM, k), lambda j, i: (i, 0)),
            pl.BlockSpec((k, TN), lambda j, i: (0, j)),
            pl.BlockSpec((TM, TN), lambda j, i: (i, j)),
        ],
        out_specs=pl.BlockSpec((TM, TN), lambda j, i: (i, j)),
        scratch_shapes=[pltpu.VMEM((k, TN), BF16)],
        compiler_params=_params(("arbitrary", "arbitrary")),
        name="proj_residual",
    )(h, w, x)


def _conv_a(abuf, sbuf, wa_ref, n_rows):
    shift_rows = A_HALO + n_rows - SUBLANES
    for s in range(1, SUBLANES):
        sbuf[s - 1, 0:shift_rows, :] = abuf[s:s + shift_rows, :]

    a0 = A_HALO - (K_A - 1)

    def tap(k, r0, cs):
        q, s = divmod(a0 + k, SUBLANES)
        src = abuf if s == 0 else sbuf.at[s - 1]
        return wa_ref[k:k + 1, cs] * src[SUBLANES * q + r0:SUBLANES * q + r0 + CONV_RB, cs]

    for c in range(MIX_C // LANES):
        cs = slice(c * LANES, (c + 1) * LANES)
        for r0 in range(0, n_rows, CONV_RB):
            acc = tap(0, r0, cs)
            for k in range(1, K_A):
                acc = acc + tap(k, r0, cs)
            yield slice(r0, r0 + CONV_RB), cs, acc


def _load_bf16_columns(w_hbm, col_starts, width, dsts, stage, sem):
    def chunk(c):
        return pltpu.make_async_copy(w_hbm.at[:, pl.ds(col_starts[c], width)], stage.at[c % 2],
                                     sem.at[c % 2])
    chunk(0).start()
    for c in range(len(col_starts)):
        if c + 1 < len(col_starts):
            chunk(c + 1).start()
        chunk(c).wait()
        dsts[c][...] = stage[c % 2].astype(BF16)


def _mixer_in_kernel(xn_ref, w_hbm, wa_ref, ba_ref, wb_ref, sa_ref, sb_ref,
                     ac_ref, hb_ref, nap_ref, nbp_ref, nas_ref, nbs_ref,
                     wbf, abuf, sbuf, bbuf, stage, sem, *, n_prompt_tiles, tiles_per_seq):
    c_tile = pl.program_id(0)
    r = pl.program_id(1)

    @pl.when(r == 0)
    def _():
        col = pl.multiple_of(c_tile * MIX_C, MIX_C)
        _load_bf16_columns(w_hbm, [g * D_A + col for g in range(5)], MIX_C,
                           [wbf.at[g] for g in range(5)], stage, sem)

    def project(xn):
        return [jnp.dot(xn, wbf[g], preferred_element_type=F32) for g in range(5)]

    @pl.when(r < n_prompt_tiles)
    def _():
        @pl.when(lax.rem(r, tiles_per_seq) == 0)
        def _():
            abuf[0:A_HALO, :] = jnp.zeros((A_HALO, MIX_C), F32)
            bbuf[0:B_HALO, :] = jnp.zeros((B_HALO, MIX_C), F32)

        pa, ga, bg, cg, hb = project(xn_ref[...])
        abuf[A_HALO:A_HALO + MIX_T, :] = pa * jax.nn.sigmoid(ga)
        bbuf[B_HALO:B_HALO + MIX_T, :] = cg * hb

        for rs, cs, acc in _conv_a(abuf, sbuf, wa_ref, MIX_T):
            ac_ref[rs, cs] = acc + ba_ref[:, cs]

        b0 = B_HALO - (K_B - 1)
        b_c = wb_ref[0:1, :] * bbuf[b0:b0 + MIX_T, :]
        for k in range(1, K_B):
            b_c = b_c + wb_ref[k:k + 1, :] * bbuf[b0 + k:b0 + k + MIX_T, :]
        hb_ref[...] = (bg * b_c).astype(BF16)

        @pl.when(lax.rem(r, tiles_per_seq) == tiles_per_seq - 1)
        def _():
            nap_ref[0] = abuf[MIX_T + A_HALO - (K_A - 1):MIX_T + A_HALO, :]
            nbp_ref[0] = bbuf[MIX_T + b0:MIX_T + B_HALO, :]

        abuf[0:A_HALO, :] = abuf[MIX_T:MIX_T + A_HALO, :]
        bbuf[0:B_HALO, :] = bbuf[MIX_T:MIX_T + B_HALO, :]

    @pl.when(r >= n_prompt_tiles)
    def _():
        rs = pl.ds(pl.multiple_of((r - n_prompt_tiles) * SAMPLE_BS, SAMPLE_BS), SAMPLE_BS)
        pa, ga, bg, cg, hb = project(xn_ref[rs, :])
        a = pa * jax.nn.sigmoid(ga)
        acc = wa_ref[K_A - 1:K_A, :] * a
        for k in range(K_A - 1):
            acc = acc + wa_ref[k:k + 1, :] * sa_ref[:, k, :]
        ac_ref[rs, :] = acc + ba_ref[...]
        for k in range(K_A - 2):
            nas_ref[:, k, :] = sa_ref[:, k + 1, :]
        nas_ref[:, K_A - 2, :] = a

        ch = cg * hb
        b_c = wb_ref[K_B - 1:K_B, :] * ch
        for k in range(K_B - 1):
            b_c = b_c + wb_ref[k:k + 1, :] * sb_ref[:, k, :]
        hb_ref[rs, :] = (bg * b_c).astype(BF16)
        for k in range(K_B - 2):
            nbs_ref[:, k, :] = sb_ref[:, k + 1, :]
        nbs_ref[:, K_B - 2, :] = ch


def _mixer_in(xn, w_in, wa, ba, wb, sa, sb, batch, seq):
    rows = xn.shape[0]
    n_sample = sa.shape[0]
    n_prompt_tiles = batch * seq // MIX_T
    tiles_per_seq = seq // MIX_T
    assert rows == batch * seq + n_sample and n_sample <= MIX_T and seq % MIX_T == 0
    assert n_sample % SAMPLE_BS == 0
    row_blk = lambda r: jnp.minimum(r, n_prompt_tiles)
    smp_blk = lambda r: jnp.maximum(r - n_prompt_tiles, 0)
    seq_of = lambda r: jnp.minimum(r, n_prompt_tiles - 1) // tiles_per_seq
    chan = lambda k: pl.BlockSpec((k, MIX_C), lambda c, r: (0, c))
    state = lambda k: pl.BlockSpec((SAMPLE_BS, k, MIX_C), lambda c, r: (smp_blk(r), 0, c))
    prompt_state = lambda k: pl.BlockSpec((1, k, MIX_C), lambda c, r: (seq_of(r), 0, c))
    return pl.pallas_call(
        functools.partial(_mixer_in_kernel, n_prompt_tiles=n_prompt_tiles,
                          tiles_per_seq=tiles_per_seq),
        out_shape=(jax.ShapeDtypeStruct((rows, D_A), F32),
                   jax.ShapeDtypeStruct((rows, D_B), BF16),
                   jax.ShapeDtypeStruct((batch, K_A - 1, D_A), F32),
                   jax.ShapeDtypeStruct((batch, K_B - 1, D_B), F32),
                   jax.ShapeDtypeStruct((n_sample, K_A - 1, D_A), F32),
                   jax.ShapeDtypeStruct((n_sample, K_B - 1, D_B), F32)),
        grid=(D_A // MIX_C, n_prompt_tiles + n_sample // SAMPLE_BS),
        in_specs=[pl.BlockSpec((MIX_T, D_MODEL), lambda c, r: (row_blk(r), 0)),
                  pl.BlockSpec(memory_space=pl.ANY),
                  chan(K_A), chan(1), chan(K_B), state(K_A - 1), state(K_B - 1)],
        out_specs=(pl.BlockSpec((MIX_T, MIX_C), lambda c, r: (row_blk(r), c)),
                   pl.BlockSpec((MIX_T, MIX_C), lambda c, r: (row_blk(r), c)),
                   prompt_state(K_A - 1), prompt_state(K_B - 1),
                   state(K_A - 1), state(K_B - 1)),
        scratch_shapes=[pltpu.VMEM((5, D_MODEL, MIX_C), BF16),
                        pltpu.VMEM((A_HALO + MIX_T, MIX_C), F32),
                        pltpu.VMEM((SUBLANES - 1, A_SHIFT_ROWS, MIX_C), F32),
                        pltpu.VMEM((B_HALO + MIX_T, MIX_C), F32),
                        pltpu.VMEM((2, D_MODEL, MIX_C), F32),
                        pltpu.SemaphoreType.DMA((2,))],
        compiler_params=_params(("arbitrary", "arbitrary")),
        name="mixer_in",
    )(xn, w_in, wa, ba, wb, sa, sb)


def _mixer_out_kernel(ac_ref, hb_ref, lg_ref, lb_ref, w_hbm, x_ref, o_ref, wb_ref, stage, sem):
    j = pl.program_id(0)

    @pl.when(pl.program_id(1) == 0)
    def _():
        col = pl.multiple_of(j * TN, TN)
        offs = range(0, TN, W_STAGE)
        _load_bf16_columns(w_hbm, [col + o for o in offs], W_STAGE,
                           [wb_ref.at[:, pl.ds(o, W_STAGE)] for o in offs], stage, sem)

    a_ln = _layernorm_rows(ac_ref[...], lg_ref[...], lb_ref[...])
    h_a = (a_ln * jax.nn.sigmoid(a_ln)).astype(BF16)
    y = jnp.dot(h_a, wb_ref[0:D_A, :], preferred_element_type=F32)
    y = y + jnp.dot(hb_ref[...], wb_ref[D_A:D_A + D_B, :], preferred_element_type=F32)
    o_ref[...] = x_ref[...] + y


def _mixer_out(ac, hb, lg, lb, w, x):
    rows = x.shape[0]
    return pl.pallas_call(
        _mixer_out_kernel,
        out_shape=jax.ShapeDtypeStruct((rows, D_MODEL), F32),
        grid=(D_MODEL // TN, rows // TM),
        in_specs=[
            pl.BlockSpec((TM, D_A), lambda j, i: (i, 0)),
            pl.BlockSpec((TM, D_B), lambda j, i: (i, 0)),
            pl.BlockSpec((1, D_A), lambda j, i: (0, 0)),
            pl.BlockSpec((1, D_A), lambda j, i: (0, 0)),
            pl.BlockSpec(memory_space=pl.ANY),
            pl.BlockSpec((TM, TN), lambda j, i: (i, j)),
        ],
        out_specs=pl.BlockSpec((TM, TN), lambda j, i: (i, j)),
        scratch_shapes=[pltpu.VMEM((D_A + D_B, TN), BF16),
                        pltpu.VMEM((2, D_A + D_B, W_STAGE), F32),
                        pltpu.SemaphoreType.DMA((2,))],
        compiler_params=_params(("arbitrary", "arbitrary")),
        name="mixer_out",
    )(ac, hb, lg, lb, w, x)


def _gate_kernel(z_ref, lg_ref, lb_ref, ws_ref, bs_ref, o_ref, v_ref, *, n_prompt_chunks):
    i = pl.program_id(0)
    row = lax.broadcasted_iota(jnp.int32, (CHUNK, CHUNK), 0)
    col = lax.broadcasted_iota(jnp.int32, (CHUNK, CHUNK), 1)
    for c in range(GATE_CHUNKS):
        rs = slice(c * CHUNK, (c + 1) * CHUNK)
        is_sample = i * GATE_CHUNKS + c >= n_prompt_chunks
        sel = jnp.where(is_sample, 1, 0)
        vn = _layernorm_rows(z_ref[rs, D_C:2 * D_C], lg_ref[...], lb_ref[...])
        vb = vn.astype(BF16)
        keep = (col <= row) & (col >= jnp.where(is_sample, row, 0))
        for h in range(H_C):
            hs = slice(h * DH_C, (h + 1) * DH_C)
            w = jnp.where(keep, ws_ref[sel, h], 0.0).astype(BF16)
            s = jnp.dot(w, vb[:, hs], preferred_element_type=F32) + bs_ref[sel][:, h:h + 1]
            o_ref[rs, hs] = (z_ref[rs, hs] * s).astype(BF16)

        if c == GATE_CHUNKS - 1:
            @pl.when(i == pl.num_programs(0) - 1)
            def _():
                v_ref[...] = vn


def _gate(z, lg, lb, ws_all, bs_all, n_prompt_chunks):
    rows = z.shape[0]
    tile = GATE_CHUNKS * CHUNK
    assert rows % tile == 0 and rows // CHUNK == n_prompt_chunks + 1
    return pl.pallas_call(
        functools.partial(_gate_kernel, n_prompt_chunks=n_prompt_chunks),
        out_shape=(jax.ShapeDtypeStruct((rows, D_C), BF16),
                   jax.ShapeDtypeStruct((CHUNK, D_C), F32)),
        grid=(rows // tile,),
        in_specs=[
            pl.BlockSpec((tile, 2 * D_C), lambda i: (i, 0)),
            pl.BlockSpec((1, D_C), lambda i: (0, 0)),
            pl.BlockSpec((1, D_C), lambda i: (0, 0)),
            pl.BlockSpec((2, H_C, CHUNK, CHUNK), lambda i: (0, 0, 0, 0)),
            pl.BlockSpec((2, CHUNK, H_C), lambda i: (0, 0, 0)),
        ],
        out_specs=(pl.BlockSpec((tile, D_C), lambda i: (i, 0)),
                   pl.BlockSpec((CHUNK, D_C), lambda i: (0, 0))),
        compiler_params=_params(("arbitrary",)),
        name="gate",
    )(z, lg, lb, ws_all, bs_all)


def kernel(x_prompt, x_sample, state_conv_a, state_conv_b, norm_g, ffn_w1, ffn_w3, ffn_w2,
           ab_w_in, a_conv_w, a_conv_b, a_ln_g, a_ln_b, b_conv_w, ab_w_out,
           c_w_in, c_b_in, c_ln_g, c_ln_b, c_w_s, c_b_s, c_w_out, final_g):
    batch, seq, _ = x_prompt.shape
    nb, dec_seq, _ = x_sample.shape
    depth = norm_g.shape[0]
    n_prompt = batch * seq
    rows = n_prompt + nb
    assert dec_seq == 1 and nb == CHUNK and n_prompt % CHUNK == 0

    row = lambda v: v.reshape(1, -1)
    ffn_stacked = (ffn_w1, ffn_w3, ffn_w2)
    w_bf16 = [w[0, 0].astype(BF16) for w in ffn_stacked]

    xs = [x_prompt.reshape(n_prompt, D_MODEL), x_sample.reshape(nb, D_MODEL)]
    new_a_p, new_b_p, new_a_s, new_b_s, new_v_s = [], [], [], [], []
    for i in range(depth):
        j = i // 2
        (x,), xn, w_bf16 = _ffn(xs, row(norm_g[i, 0]), *w_bf16, row(norm_g[i, 1]), (rows,),
                                next_w=(ffn_stacked, (i, 1)), emit_norm=True)
        if i % 2 == 0:
            ac, hb, sa_p, sb_p, sa_s, sb_s = _mixer_in(
                xn, ab_w_in[j], a_conv_w[j], row(a_conv_b[j]), b_conv_w[j],
                state_conv_a[j], state_conv_b[j], batch, seq)
            new_a_p.append(sa_p)
            new_b_p.append(sb_p)
            new_a_s.append(sa_s)
            new_b_s.append(sb_s)
            x = _mixer_out(ac, hb, row(a_ln_g[j]), row(a_ln_b[j]), ab_w_out[j], x)
        else:
            z = _proj(xn, c_w_in[j], row(c_b_in[j]), True)
            w_s = c_w_s[j][:, :CHUNK, :CHUNK]
            ws_all = jnp.stack([w_s, jnp.broadcast_to(w_s[:, :1, :1], w_s.shape)])
            b_s = c_b_s[j][:, :CHUNK].T
            bs_all = jnp.stack([b_s, jnp.broadcast_to(b_s[:1], b_s.shape)])
            h, v_s = _gate(z, row(c_ln_g[j]), row(c_ln_b[j]), ws_all, bs_all, n_prompt // CHUNK)
            new_v_s.append(v_s.reshape(nb, 1, D_C))
            x = _proj_residual(h, c_w_out[j], x)
        last = i == depth - 1
        xs, _, w_bf16 = _ffn([x], row(norm_g[i, 2]), *w_bf16, row(final_g),
                             (n_prompt, nb) if last else (rows,),
                             next_w=None if last else (ffn_stacked, (i + 1, 0)), final_norm=last)

    y_prompt, y_sample = xs
    return (y_prompt.reshape(batch, seq, D_MODEL), y_sample.reshape(nb, 1, D_MODEL),
            jnp.stack(new_a_p), jnp.stack(new_b_p), jnp.stack(new_a_s), jnp.stack(new_b_s),
            jnp.stack(new_v_s))
```

```python
import functools

import jax
import jax.numpy as jnp
from jax import lax
from jax.experimental import pallas as pl
from jax.experimental.pallas import tpu as pltpu

F32 = jnp.float32
BF16 = jnp.bfloat16

D_MODEL = 2048
D_FF = 5632
D_A = 1024
D_B = 1024
K_A = 31
K_B = 3
D_C = 2048
H_C = 8
DH_C = D_C // H_C
CHUNK = 128
EPS = 1e-6

VMEM_LIMIT_BYTES = 56 * 1024 * 1024
LANES = 128
SUBLANES = 8

TM = 1040
NORM_ROWS = 208
TF = 512
TN = 1024
W_STAGE = 256
MIX_T = 1024
MIX_C = 256
SAMPLE_BS = 32
CONV_RB = 64
A_HALO = 32
B_HALO = 8
A_SHIFT_ROWS = A_HALO + MIX_T - SUBLANES
GATE_CHUNKS = 5


def _params(sem):
    return pltpu.CompilerParams(dimension_semantics=sem, vmem_limit_bytes=VMEM_LIMIT_BYTES)


def _rms_rows(x, g):
    ms = jnp.mean(x * x, axis=-1, keepdims=True)
    return x * lax.rsqrt(ms + EPS) * g


def _layernorm_rows(x, g, b):
    mu = jnp.mean(x, axis=-1, keepdims=True)
    xc = x - mu
    var = jnp.mean(xc * xc, axis=-1, keepdims=True)
    return xc * lax.rsqrt(var + EPS) * g + b


def _norm_chunks(src_ref, g_ref, dst_ref, dtype):
    def body(c, carry):
        r = pl.ds(pl.multiple_of(c * NORM_ROWS, NORM_ROWS), NORM_ROWS)
        dst_ref[r, :] = _rms_rows(src_ref[r, :], g_ref[...]).astype(dtype)
        return carry
    lax.fori_loop(0, TM // NORM_ROWS, body, 0)


def _tile_pieces(tile, seg_rows):
    lo, hi = tile * TM, (tile + 1) * TM
    pieces, base = [], 0
    for k, n in enumerate(seg_rows):
        a, b = max(lo, base), min(hi, base + n)
        if a < b:
            pieces.append((k, a - base, a - lo, b - a))
        base += n
    return pieces


def _ffn_kernel(*refs, in_rows, out_rows, convert_next, emit_norm, final_norm):
    n_in, n_out = len(in_rows), len(out_rows)
    it = iter(refs)
    x_srcs = [next(it) for _ in range(n_in)]
    g_ref, w1_ref, w3_ref, w2_ref, post_g_ref = (next(it) for _ in range(5))
    next_w = [next(it) for _ in range(3)] if convert_next else []
    o_dsts = [next(it) for _ in range(n_out)]
    xn_out = next(it) if emit_norm else None
    next_w_out = [next(it) for _ in range(3)] if convert_next else []
    acc_ref, xn_ref, load_sem, store_sem = (next(it) for _ in range(4))
    xn_next_ref, xn_sem = (next(it), next(it)) if emit_norm else (None, None)

    i = pl.program_id(0)
    j = pl.program_id(1)
    n_i = pl.num_programs(0)
    n_j = pl.num_programs(1)
    n_tiles = sum(in_rows) // TM

    def load_copies(t):
        return [pltpu.make_async_copy(x_srcs[k].at[pl.ds(r0, n)],
                                      acc_ref.at[t % 2, pl.ds(d0, n)], load_sem.at[t % 2])
                for k, r0, d0, n in _tile_pieces(t, in_rows)]

    def store_copies(t):
        return [pltpu.make_async_copy(acc_ref.at[t % 2, pl.ds(d0, n)],
                                      o_dsts[k].at[pl.ds(r0, n)], store_sem.at[t % 2])
                for k, r0, d0, n in _tile_pieces(t, out_rows)]

    def xn_copy(t):
        return pltpu.make_async_copy(xn_next_ref, xn_out.at[pl.ds(t * TM, TM)], xn_sem)

    def on_tile(tile, fn):
        for t in range(n_tiles):
            @pl.when(tile == t)
            def _(t=t):
                fn(t)

    acc = acc_ref.at[lax.rem(i, 2)]

    @pl.when(j == 0)
    def _():
        @pl.when(i == 0)
        def _():
            for cp in load_copies(0):
                cp.start()
        on_tile(i, lambda t: [cp.wait() for cp in load_copies(t)])
        _norm_chunks(acc, g_ref, xn_ref, BF16)

    @pl.when(j == 2)
    def _():
        @pl.when(i > 0)
        def _():
            on_tile(i - 1, lambda t: [cp.wait() for cp in store_copies(t)])
            if emit_norm:
                on_tile(i - 1, lambda t: xn_copy(t).wait())

        @pl.when(i + 1 < n_i)
        def _():
            on_tile(i + 1, lambda t: [cp.start() for cp in load_copies(t)])

    xn = xn_ref[...]
    h1 = jnp.dot(xn, w1_ref[...], preferred_element_type=F32)
    h3 = jnp.dot(xn, w3_ref[...], preferred_element_type=F32)
    h = (0.5 * (h1 * jax.nn.sigmoid(h1)) * h3).astype(BF16)
    for n in range(D_MODEL // TF):
        cs = slice(n * TF, (n + 1) * TF)
        acc[:, cs] += jnp.dot(h, w2_ref[:, cs], preferred_element_type=F32)

    if convert_next:
        for src, dst in zip(next_w, next_w_out):
            dst[...] = src[...].astype(BF16)

    @pl.when(j == n_j - 1)
    def _():
        if final_norm:
            _norm_chunks(acc, post_g_ref, acc, F32)
        on_tile(i, lambda t: [cp.start() for cp in store_copies(t)])
        if emit_norm:
            _norm_chunks(acc, post_g_ref, xn_next_ref, BF16)
            on_tile(i, lambda t: xn_copy(t).start())

        @pl.when(i == n_i - 1)
        def _():
            on_tile(i, lambda t: [cp.wait() for cp in store_copies(t)])
            if emit_norm:
                on_tile(i, lambda t: xn_copy(t).wait())


def _ffn(xs, g, w1, w3, w2, post_g, out_rows, next_w=None, emit_norm=False, final_norm=False):
    in_rows = tuple(a.shape[0] for a in xs)
    rows = sum(in_rows)
    assert rows == sum(out_rows) and rows % TM == 0 and D_FF % TF == 0
    n_i, n_j = rows // TM, D_FF // TF
    assert n_j > 2 and D_MODEL % n_i == 0 and TF % n_i == 0
    convert_next = next_w is not None
    any_spec = pl.BlockSpec(memory_space=pl.ANY)
    row_spec = pl.BlockSpec((1, D_MODEL), lambda i, j: (0, 0))
    up_piece = pl.BlockSpec((None, D_MODEL // n_i, TF), lambda i, j: (j, i, 0))
    down_piece = pl.BlockSpec((TF // n_i, D_MODEL), lambda i, j: (j * n_i + i, 0))

    in_specs = [any_spec] * len(xs) + [
        row_spec,
        pl.BlockSpec((None, D_MODEL, TF), lambda i, j: (j, 0, 0)),
        pl.BlockSpec((None, D_MODEL, TF), lambda i, j: (j, 0, 0)),
        pl.BlockSpec((TF, D_MODEL), lambda i, j: (j, 0)),
        row_spec,
    ]
    args = list(xs) + [g, w1, w3, w2, post_g]
    out_shape = [jax.ShapeDtypeStruct((n, D_MODEL), F32) for n in out_rows]
    out_specs = [any_spec] * len(out_rows)
    if emit_norm:
        out_shape.append(jax.ShapeDtypeStruct((rows, D_MODEL), BF16))
        out_specs.append(any_spec)
    if convert_next:
        stacked, (li, si) = next_w
        in_specs += [
            pl.BlockSpec((None, None, D_MODEL // n_i, TF), lambda i, j: (li, si, i, j)),
            pl.BlockSpec((None, None, D_MODEL // n_i, TF), lambda i, j: (li, si, i, j)),
            pl.BlockSpec((None, None, TF // n_i, D_MODEL), lambda i, j: (li, si, j * n_i + i, 0)),
        ]
        args += list(stacked)
        out_shape += [jax.ShapeDtypeStruct((n_j, D_MODEL, TF), BF16)] * 2
        out_shape += [jax.ShapeDtypeStruct((D_FF, D_MODEL), BF16)]
        out_specs += [up_piece, up_piece, down_piece]
    scratch = [pltpu.VMEM((2, TM, D_MODEL), F32), pltpu.VMEM((TM, D_MODEL), BF16),
               pltpu.SemaphoreType.DMA((2,)), pltpu.SemaphoreType.DMA((2,))]
    if emit_norm:
        scratch += [pltpu.VMEM((TM, D_MODEL), BF16), pltpu.SemaphoreType.DMA(())]

    res = pl.pallas_call(
        functools.partial(_ffn_kernel, in_rows=in_rows, out_rows=tuple(out_rows),
                          convert_next=convert_next, emit_norm=emit_norm, final_norm=final_norm),
        out_shape=out_shape,
        grid=(n_i, n_j),
        in_specs=in_specs,
        out_specs=out_specs,
        scratch_shapes=scratch,
        compiler_params=_params(("arbitrary", "arbitrary")),
        name="ffn",
    )(*args)
    res = list(res)
    outs = [res.pop(0) for _ in out_rows]
    xn = res.pop(0) if emit_norm else None
    return outs, xn, (res if convert_next else None)


def _proj_kernel(a_ref, w_ref, b_ref, o_ref, wb_ref, *, gelu):
    @pl.when(pl.program_id(1) == 0)
    def _():
        wb_ref[...] = w_ref[...].astype(BF16)

    y = jnp.dot(a_ref[...], wb_ref[...], preferred_element_type=F32)
    if gelu:
        y = y + b_ref[...]
        y = 0.5 * y * (1.0 + lax.erf(y * (2.0 ** -0.5)))
    o_ref[...] = y


def _proj(a, w, b, gelu):
    rows, k = a.shape
    n = w.shape[1]
    return pl.pallas_call(
        functools.partial(_proj_kernel, gelu=gelu),
        out_shape=jax.ShapeDtypeStruct((rows, n), F32),
        grid=(n // TN, rows // TM),
        in_specs=[
            pl.BlockSpec((TM, k), lambda j, i: (i, 0)),
            pl.BlockSpec((k, TN), lambda j, i: (0, j)),
            pl.BlockSpec((1, TN), lambda j, i: (0, j)),
        ],
        out_specs=pl.BlockSpec((TM, TN), lambda j, i: (i, j)),
        scratch_shapes=[pltpu.VMEM((k, TN), BF16)],
        compiler_params=_params(("arbitrary", "arbitrary")),
        name="proj_gelu" if gelu else "proj",
    )(a, w, b)


def _proj_residual_kernel(h_ref, w_ref, x_ref, o_ref, wb_ref):
    @pl.when(pl.program_id(1) == 0)
    def _():
        wb_ref[...] = w_ref[...].astype(BF16)

    o_ref[...] = x_ref[...] + jnp.dot(h_ref[...], wb_ref[...], preferred_element_type=F32)


def _proj_residual(h, w, x):
    rows, k = h.shape
    return pl.pallas_call(
        _proj_residual_kernel,
        out_shape=jax.ShapeDtypeStruct((rows, D_MODEL), F32),
        grid=(D_MODEL // TN, rows // TM),
        in_specs=[
            pl.BlockSpec((TM, k), lambda j, i: (i, 0)),
            pl.BlockSpec((k, TN), lambda j, i: (0, j)),
            pl.BlockSpec((TM, TN), lambda j, i: (i, j)),
        ],
        out_specs=pl.BlockSpec((TM, TN), lambda j, i: (i, j)),
        scratch_shapes=[pltpu.VMEM((k, TN), BF16)],
        compiler_params=_params(("arbitrary", "arbitrary")),
        name="proj_residual",
    )(h, w, x)


def _conv_a(abuf, sbuf, wa_ref, n_rows):
    shift_rows = A_HALO + n_rows - SUBLANES
    for s in range(1, SUBLANES):
        sbuf[s - 1, 0:shift_rows, :] = abuf[s:s + shift_rows, :]

    a0 = A_HALO - (K_A - 1)

    def tap(k, r0, cs):
        q, s = divmod(a0 + k, SUBLANES)
        src = abuf if s == 0 else sbuf.at[s - 1]
        return wa_ref[k:k + 1, cs] * src[SUBLANES * q + r0:SUBLANES * q + r0 + CONV_RB, cs]

    for c in range(MIX_C // LANES):
        cs = slice(c * LANES, (c + 1) * LANES)
        for r0 in range(0, n_rows, CONV_RB):
            acc = tap(0, r0, cs)
            for k in range(1, K_A):
                acc = acc + tap(k, r0, cs)
            yield slice(r0, r0 + CONV_RB), cs, acc


def _load_bf16_columns(w_hbm, col_starts, width, dsts, stage, sem):
    def chunk(c):
        return pltpu.make_async_copy(w_hbm.at[:, pl.ds(col_starts[c], width)], stage.at[c % 2],
                                     sem.at[c % 2])
    chunk(0).start()
    for c in range(len(col_starts)):
        if c + 1 < len(col_starts):
            chunk(c + 1).start()
        chunk(c).wait()
        dsts[c][...] = stage[c % 2].astype(BF16)


def _mixer_in_kernel(xn_ref, w_hbm, wa_ref, ba_ref, wb_ref, sa_ref, sb_ref,
                     ac_ref, hb_ref, nap_ref, nbp_ref, nas_ref, nbs_ref,
                     wbf, abuf, sbuf, bbuf, stage, sem, *, n_prompt_tiles, tiles_per_seq):
    c_tile = pl.program_id(0)
    r = pl.program_id(1)

    @pl.when(r == 0)
    def _():
        col = pl.multiple_of(c_tile * MIX_C, MIX_C)
        _load_bf16_columns(w_hbm, [g * D_A + col for g in range(5)], MIX_C,
                           [wbf.at[g] for g in range(5)], stage, sem)

    def project(xn):
        return [jnp.dot(xn, wbf[g], preferred_element_type=F32) for g in range(5)]

    @pl.when(r < n_prompt_tiles)
    def _():
        @pl.when(lax.rem(r, tiles_per_seq) == 0)
        def _():
            abuf[0:A_HALO, :] = jnp.zeros((A_HALO, MIX_C), F32)
            bbuf[0:B_HALO, :] = jnp.zeros((B_HALO, MIX_C), F32)

        pa, ga, bg, cg, hb = project(xn_ref[...])
        abuf[A_HALO:A_HALO + MIX_T, :] = pa * jax.nn.sigmoid(ga)
        bbuf[B_HALO:B_HALO + MIX_T, :] = cg * hb

        for rs, cs, acc in _conv_a(abuf, sbuf, wa_ref, MIX_T):
            ac_ref[rs, cs] = acc + ba_ref[:, cs]

        b0 = B_HALO - (K_B - 1)
        b_c = wb_ref[0:1, :] * bbuf[b0:b0 + MIX_T, :]
        for k in range(1, K_B):
            b_c = b_c + wb_ref[k:k + 1, :] * bbuf[b0 + k:b0 + k + MIX_T, :]
        hb_ref[...] = (bg * b_c).astype(BF16)

        @pl.when(lax.rem(r, tiles_per_seq) == tiles_per_seq - 1)
        def _():
            nap_ref[0] = abuf[MIX_T + A_HALO - (K_A - 1):MIX_T + A_HALO, :]
            nbp_ref[0] = bbuf[MIX_T + b0:MIX_T + B_HALO, :]

        abuf[0:A_HALO, :] = abuf[MIX_T:MIX_T + A_HALO, :]
        bbuf[0:B_HALO, :] = bbuf[MIX_T:MIX_T + B_HALO, :]

    @pl.when(r >= n_prompt_tiles)
    def _():
        rs = pl.ds(pl.multiple_of((r - n_prompt_tiles) * SAMPLE_BS, SAMPLE_BS), SAMPLE_BS)
        pa, ga, bg, cg, hb = project(xn_ref[rs, :])
        a = pa * jax.nn.sigmoid(ga)
        acc = wa_ref[K_A - 1:K_A, :] * a
        for k in range(K_A - 1):
            acc = acc + wa_ref[k:k + 1, :] * sa_ref[:, k, :]
        ac_ref[rs, :] = acc + ba_ref[...]
        for k in range(K_A - 2):
            nas_ref[:, k, :] = sa_ref[:, k + 1, :]
        nas_ref[:, K_A - 2, :] = a

        ch = cg * hb
        b_c = wb_ref[K_B - 1:K_B, :] * ch
        for k in range(K_B - 1):
            b_c = b_c + wb_ref[k:k + 1, :] * sb_ref[:, k, :]
        hb_ref[rs, :] = (bg * b_c).astype(BF16)
        for k in range(K_B - 2):
            nbs_ref[:, k, :] = sb_ref[:, k + 1, :]
        nbs_ref[:, K_B - 2, :] = ch


def _mixer_in(xn, w_in, wa, ba, wb, sa, sb, batch, seq):
    rows = xn.shape[0]
    n_sample = sa.shape[0]
    n_prompt_tiles = batch * seq // MIX_T
    tiles_per_seq = seq // MIX_T
    assert rows == batch * seq + n_sample and n_sample <= MIX_T and seq % MIX_T == 0
    assert n_sample % SAMPLE_BS == 0
    row_blk = lambda r: jnp.minimum(r, n_prompt_tiles)
    smp_blk = lambda r: jnp.maximum(r - n_prompt_tiles, 0)
    seq_of = lambda r: jnp.minimum(r, n_prompt_tiles - 1) // tiles_per_seq
    chan = lambda k: pl.BlockSpec((k, MIX_C), lambda c, r: (0, c))
    state = lambda k: pl.BlockSpec((SAMPLE_BS, k, MIX_C), lambda c, r: (smp_blk(r), 0, c))
    new_state = lambda k: pl.BlockSpec((None, SAMPLE_BS, k, MIX_C),
                                       lambda c, r: (0, smp_blk(r), 0, c))
    prompt_state = lambda k: pl.BlockSpec((None, 1, k, MIX_C), lambda c, r: (0, seq_of(r), 0, c))
    return pl.pallas_call(
        functools.partial(_mixer_in_kernel, n_prompt_tiles=n_prompt_tiles,
                          tiles_per_seq=tiles_per_seq),
        out_shape=(jax.ShapeDtypeStruct((rows, D_A), F32),
                   jax.ShapeDtypeStruct((rows, D_B), BF16),
                   jax.ShapeDtypeStruct((1, batch, K_A - 1, D_A), F32),
                   jax.ShapeDtypeStruct((1, batch, K_B - 1, D_B), F32),
                   jax.ShapeDtypeStruct((1, n_sample, K_A - 1, D_A), F32),
                   jax.ShapeDtypeStruct((1, n_sample, K_B - 1, D_B), F32)),
        grid=(D_A // MIX_C, n_prompt_tiles + n_sample // SAMPLE_BS),
        in_specs=[pl.BlockSpec((MIX_T, D_MODEL), lambda c, r: (row_blk(r), 0)),
                  pl.BlockSpec(memory_space=pl.ANY),
                  chan(K_A), chan(1), chan(K_B), state(K_A - 1), state(K_B - 1)],
        out_specs=(pl.BlockSpec((MIX_T, MIX_C), lambda c, r: (row_blk(r), c)),
                   pl.BlockSpec((MIX_T, MIX_C), lambda c, r: (row_blk(r), c)),
                   prompt_state(K_A - 1), prompt_state(K_B - 1),
                   new_state(K_A - 1), new_state(K_B - 1)),
        scratch_shapes=[pltpu.VMEM((5, D_MODEL, MIX_C), BF16),
                        pltpu.VMEM((A_HALO + MIX_T, MIX_C), F32),
                        pltpu.VMEM((SUBLANES - 1, A_SHIFT_ROWS, MIX_C), F32),
                        pltpu.VMEM((B_HALO + MIX_T, MIX_C), F32),
                        pltpu.VMEM((2, D_MODEL, MIX_C), F32),
                        pltpu.SemaphoreType.DMA((2,))],
        compiler_params=_params(("arbitrary", "arbitrary")),
        name="mixer_in",
    )(xn, w_in, wa, ba, wb, sa, sb)


def _mixer_out_kernel(ac_ref, hb_ref, lg_ref, lb_ref, w_hbm, x_ref, o_ref, wb_ref, stage, sem):
    j = pl.program_id(0)

    @pl.when(pl.program_id(1) == 0)
    def _():
        col = pl.multiple_of(j * TN, TN)
        offs = range(0, TN, W_STAGE)
        _load_bf16_columns(w_hbm, [col + o for o in offs], W_STAGE,
                           [wb_ref.at[:, pl.ds(o, W_STAGE)] for o in offs], stage, sem)

    a_ln = _layernorm_rows(ac_ref[...], lg_ref[...], lb_ref[...])
    h_a = (a_ln * jax.nn.sigmoid(a_ln)).astype(BF16)
    y = jnp.dot(h_a, wb_ref[0:D_A, :], preferred_element_type=F32)
    y = y + jnp.dot(hb_ref[...], wb_ref[D_A:D_A + D_B, :], preferred_element_type=F32)
    o_ref[...] = x_ref[...] + y


def _mixer_out(ac, hb, lg, lb, w, x):
    rows = x.shape[0]
    return pl.pallas_call(
        _mixer_out_kernel,
        out_shape=jax.ShapeDtypeStruct((rows, D_MODEL), F32),
        grid=(D_MODEL // TN, rows // TM),
        in_specs=[
            pl.BlockSpec((TM, D_A), lambda j, i: (i, 0)),
            pl.BlockSpec((TM, D_B), lambda j, i: (i, 0)),
            pl.BlockSpec((1, D_A), lambda j, i: (0, 0)),
            pl.BlockSpec((1, D_A), lambda j, i: (0, 0)),
            pl.BlockSpec(memory_space=pl.ANY),
            pl.BlockSpec((TM, TN), lambda j, i: (i, j)),
        ],
        out_specs=pl.BlockSpec((TM, TN), lambda j, i: (i, j)),
        scratch_shapes=[pltpu.VMEM((D_A + D_B, TN), BF16),
                        pltpu.VMEM((2, D_A + D_B, W_STAGE), F32),
                        pltpu.SemaphoreType.DMA((2,))],
        compiler_params=_params(("arbitrary", "arbitrary")),
        name="mixer_out",
    )(ac, hb, lg, lb, w, x)


def _gate_kernel(z_ref, lg_ref, lb_ref, ws_ref, bs_ref, o_ref, v_ref, *, n_prompt_chunks):
    i = pl.program_id(0)
    row = lax.broadcasted_iota(jnp.int32, (CHUNK, CHUNK), 0)
    col = lax.broadcasted_iota(jnp.int32, (CHUNK, CHUNK), 1)
    for c in range(GATE_CHUNKS):
        rs = slice(c * CHUNK, (c + 1) * CHUNK)
        is_sample = i * GATE_CHUNKS + c >= n_prompt_chunks
        sel = jnp.where(is_sample, 1, 0)
        vn = _layernorm_rows(z_ref[rs, D_C:2 * D_C], lg_ref[...], lb_ref[...])
        vb = vn.astype(BF16)
        keep = (col <= row) & (col >= jnp.where(is_sample, row, 0))
        for h in range(H_C):
            hs = slice(h * DH_C, (h + 1) * DH_C)
            w = jnp.where(keep, ws_ref[sel, h], 0.0).astype(BF16)
            s = jnp.dot(w, vb[:, hs], preferred_element_type=F32) + bs_ref[sel][:, h:h + 1]
            o_ref[rs, hs] = (z_ref[rs, hs] * s).astype(BF16)

        if c == GATE_CHUNKS - 1:
            @pl.when(i == pl.num_programs(0) - 1)
            def _():
                v_ref[...] = vn


def _gate(z, lg, lb, ws_all, bs_all, n_prompt_chunks):
    rows = z.shape[0]
    tile = GATE_CHUNKS * CHUNK
    assert rows % tile == 0 and rows // CHUNK == n_prompt_chunks + 1
    return pl.pallas_call(
        functools.partial(_gate_kernel, n_prompt_chunks=n_prompt_chunks),
        out_shape=(jax.ShapeDtypeStruct((rows, D_C), BF16),
                   jax.ShapeDtypeStruct((CHUNK, D_C), F32)),
        grid=(rows // tile,),
        in_specs=[
            pl.BlockSpec((tile, 2 * D_C), lambda i: (i, 0)),
            pl.BlockSpec((1, D_C), lambda i: (0, 0)),
            pl.BlockSpec((1, D_C), lambda i: (0, 0)),
            pl.BlockSpec((2, H_C, CHUNK, CHUNK), lambda i: (0, 0, 0, 0)),
            pl.BlockSpec((2, CHUNK, H_C), lambda i: (0, 0, 0)),
        ],
        out_specs=(pl.BlockSpec((tile, D_C), lambda i: (i, 0)),
                   pl.BlockSpec((CHUNK, D_C), lambda i: (0, 0))),
        compiler_params=_params(("arbitrary",)),
        name="gate",
    )(z, lg, lb, ws_all, bs_all)


def kernel(x_prompt, x_sample, state_conv_a, state_conv_b, norm_g, ffn_w1, ffn_w3, ffn_w2,
           ab_w_in, a_conv_w, a_conv_b, a_ln_g, a_ln_b, b_conv_w, ab_w_out,
           c_w_in, c_b_in, c_ln_g, c_ln_b, c_w_s, c_b_s, c_w_out, final_g):
    batch, seq, _ = x_prompt.shape
    nb, dec_seq, _ = x_sample.shape
    depth = norm_g.shape[0]
    n_prompt = batch * seq
    rows = n_prompt + nb
    assert dec_seq == 1 and nb == CHUNK and n_prompt % CHUNK == 0

    row = lambda v: v.reshape(1, -1)
    ffn_stacked = (ffn_w1, ffn_w3, ffn_w2)
    up_blocks = lambda w: w.astype(BF16).reshape(D_MODEL, D_FF // TF, TF).transpose(1, 0, 2)
    w_bf16 = [up_blocks(ffn_w1[0, 0]), up_blocks(ffn_w3[0, 0]), ffn_w2[0, 0].astype(BF16)]

    xs = [x_prompt.reshape(n_prompt, D_MODEL), x_sample.reshape(nb, D_MODEL)]
    new_a_p, new_b_p, new_a_s, new_b_s, new_v_s = [], [], [], [], []
    for i in range(depth):
        j = i // 2
        (x,), xn, w_bf16 = _ffn(xs, row(norm_g[i, 0]), *w_bf16, row(norm_g[i, 1]), (rows,),
                                next_w=(ffn_stacked, (i, 1)), emit_norm=True)
        if i % 2 == 0:
            ac, hb, sa_p, sb_p, sa_s, sb_s = _mixer_in(
                xn, ab_w_in[j], a_conv_w[j], row(a_conv_b[j]), b_conv_w[j],
                state_conv_a[j], state_conv_b[j], batch, seq)
            new_a_p.append(sa_p)
            new_b_p.append(sb_p)
            new_a_s.append(sa_s)
            new_b_s.append(sb_s)
            x = _mixer_out(ac, hb, row(a_ln_g[j]), row(a_ln_b[j]), ab_w_out[j], x)
        else:
            z = _proj(xn, c_w_in[j], row(c_b_in[j]), True)
            w_s = c_w_s[j][:, :CHUNK, :CHUNK]
            ws_all = jnp.stack([w_s, jnp.broadcast_to(w_s[:, :1, :1], w_s.shape)])
            b_s = c_b_s[j][:, :CHUNK].T
            bs_all = jnp.stack([b_s, jnp.broadcast_to(b_s[:1], b_s.shape)])
            h, v_s = _gate(z, row(c_ln_g[j]), row(c_ln_b[j]), ws_all, bs_all, n_prompt // CHUNK)
            new_v_s.append(v_s.reshape(nb, 1, D_C))
            x = _proj_residual(h, c_w_out[j], x)
        last = i == depth - 1
        xs, _, w_bf16 = _ffn([x], row(norm_g[i, 2]), *w_bf16, row(final_g),
                             (n_prompt, nb) if last else (rows,),
                             next_w=None if last else (ffn_stacked, (i + 1, 0)), final_norm=last)

    y_prompt, y_sample = xs
    return (y_prompt.reshape(batch, seq, D_MODEL), y_sample.reshape(nb, 1, D_MODEL),
            jnp.concatenate(new_a_p), jnp.concatenate(new_b_p),
            jnp.concatenate(new_a_s), jnp.concatenate(new_b_s),
            jnp.stack(new_v_s))
```

```python
import functools

import jax
import jax.numpy as jnp
from jax import lax
from jax.experimental import pallas as pl
from jax.experimental.pallas import tpu as pltpu

F32 = jnp.float32
BF16 = jnp.bfloat16

D_MODEL = 2048
D_FF = 5632
D_A = 1024
D_B = 1024
K_A = 31
K_B = 3
D_C = 2048
H_C = 8
DH_C = D_C // H_C
CHUNK = 128
EPS = 1e-6

VMEM_LIMIT_BYTES = 56 * 1024 * 1024
LANES = 128
SUBLANES = 8

TM = 1040
NORM_ROWS = 208
TF = 512
TN = 1024
W_STAGE = 256
MIX_T = 1024
MIX_C = 256
SAMPLE_BS = 64
CONV_RB = 64
A_HALO = 32
B_HALO = 8
GATE_CHUNKS = 5


def _params(sem):
    return pltpu.CompilerParams(dimension_semantics=sem, vmem_limit_bytes=VMEM_LIMIT_BYTES)


def _rms_rows(x, g):
    ms = jnp.mean(x * x, axis=-1, keepdims=True)
    return x * lax.rsqrt(ms + EPS) * g


def _layernorm_rows(x, g, b):
    mu = jnp.mean(x, axis=-1, keepdims=True)
    xc = x - mu
    var = jnp.mean(xc * xc, axis=-1, keepdims=True)
    return xc * lax.rsqrt(var + EPS) * g + b


def _norm_chunks(src_ref, g_ref, dst_ref, dtype):
    def body(c, carry):
        r = pl.ds(pl.multiple_of(c * NORM_ROWS, NORM_ROWS), NORM_ROWS)
        dst_ref[r, :] = _rms_rows(src_ref[r, :], g_ref[...]).astype(dtype)
        return carry
    lax.fori_loop(0, TM // NORM_ROWS, body, 0)


def _tile_pieces(tile, seg_rows):
    lo, hi = tile * TM, (tile + 1) * TM
    pieces, base = [], 0
    for k, n in enumerate(seg_rows):
        a, b = max(lo, base), min(hi, base + n)
        if a < b:
            pieces.append((k, a - base, a - lo, b - a))
        base += n
    return pieces


def _ffn_kernel(*refs, in_rows, out_rows, convert_next, emit_norm, final_norm):
    n_in, n_out = len(in_rows), len(out_rows)
    it = iter(refs)
    x_srcs = [next(it) for _ in range(n_in)]
    g_ref, w1_ref, w3_ref, w2_ref, post_g_ref = (next(it) for _ in range(5))
    next_w = [next(it) for _ in range(3)] if convert_next else []
    o_dsts = [next(it) for _ in range(n_out)]
    xn_out = next(it) if emit_norm else None
    next_w_out = [next(it) for _ in range(3)] if convert_next else []
    acc_ref, xn_ref, load_sem, store_sem = (next(it) for _ in range(4))
    xn_next_ref, xn_sem = (next(it), next(it)) if emit_norm else (None, None)

    i = pl.program_id(0)
    j = pl.program_id(1)
    n_i = pl.num_programs(0)
    n_j = pl.num_programs(1)
    n_tiles = sum(in_rows) // TM

    def load_copies(t):
        return [pltpu.make_async_copy(x_srcs[k].at[pl.ds(r0, n)],
                                      acc_ref.at[t % 2, pl.ds(d0, n)], load_sem.at[t % 2])
                for k, r0, d0, n in _tile_pieces(t, in_rows)]

    def store_copies(t):
        return [pltpu.make_async_copy(acc_ref.at[t % 2, pl.ds(d0, n)],
                                      o_dsts[k].at[pl.ds(r0, n)], store_sem.at[t % 2])
                for k, r0, d0, n in _tile_pieces(t, out_rows)]

    def xn_copy(t):
        return pltpu.make_async_copy(xn_next_ref, xn_out.at[pl.ds(t * TM, TM)], xn_sem)

    def on_tile(tile, fn):
        for t in range(n_tiles):
            @pl.when(tile == t)
            def _(t=t):
                fn(t)

    acc = acc_ref.at[lax.rem(i, 2)]

    @pl.when(j == 0)
    def _():
        @pl.when(i == 0)
        def _():
            for cp in load_copies(0):
                cp.start()
        on_tile(i, lambda t: [cp.wait() for cp in load_copies(t)])
        _norm_chunks(acc, g_ref, xn_ref, BF16)

    @pl.when(j == 2)
    def _():
        @pl.when(i > 0)
        def _():
            on_tile(i - 1, lambda t: [cp.wait() for cp in store_copies(t)])
            if emit_norm:
                on_tile(i - 1, lambda t: xn_copy(t).wait())

        @pl.when(i + 1 < n_i)
        def _():
            on_tile(i + 1, lambda t: [cp.start() for cp in load_copies(t)])

    xn = xn_ref[...]
    h1 = jnp.dot(xn, w1_ref[...], preferred_element_type=F32)
    h3 = jnp.dot(xn, w3_ref[...], preferred_element_type=F32)
    h = (0.5 * (h1 * jax.nn.sigmoid(h1)) * h3).astype(BF16)
    for n in range(D_MODEL // TF):
        cs = slice(n * TF, (n + 1) * TF)
        acc[:, cs] += jnp.dot(h, w2_ref[:, cs], preferred_element_type=F32)

    if convert_next:
        for src, dst in zip(next_w, next_w_out):
            dst[...] = src[...].astype(BF16)

    @pl.when(j == n_j - 1)
    def _():
        if final_norm:
            _norm_chunks(acc, post_g_ref, acc, F32)
        on_tile(i, lambda t: [cp.start() for cp in store_copies(t)])
        if emit_norm:
            _norm_chunks(acc, post_g_ref, xn_next_ref, BF16)
            on_tile(i, lambda t: xn_copy(t).start())

        @pl.when(i == n_i - 1)
        def _():
            on_tile(i, lambda t: [cp.wait() for cp in store_copies(t)])
            if emit_norm:
                on_tile(i, lambda t: xn_copy(t).wait())


def _ffn(xs, g, w1, w3, w2, post_g, out_rows, next_w=None, emit_norm=False, final_norm=False):
    in_rows = tuple(a.shape[0] for a in xs)
    rows = sum(in_rows)
    assert rows == sum(out_rows) and rows % TM == 0 and D_FF % TF == 0
    n_i, n_j = rows // TM, D_FF // TF
    assert n_j > 2 and D_MODEL % n_i == 0 and TF % n_i == 0
    convert_next = next_w is not None
    any_spec = pl.BlockSpec(memory_space=pl.ANY)
    row_spec = pl.BlockSpec((1, D_MODEL), lambda i, j: (0, 0))
    up_piece = pl.BlockSpec((D_MODEL // n_i, TF), lambda i, j: (i, j))
    down_piece = pl.BlockSpec((TF, D_MODEL // n_i), lambda i, j: (j, i))

    in_specs = [any_spec] * len(xs) + [
        row_spec,
        pl.BlockSpec((D_MODEL, TF), lambda i, j: (0, j)),
        pl.BlockSpec((D_MODEL, TF), lambda i, j: (0, j)),
        pl.BlockSpec((TF, D_MODEL), lambda i, j: (j, 0)),
        row_spec,
    ]
    args = list(xs) + [g, w1, w3, w2, post_g]
    out_shape = [jax.ShapeDtypeStruct((n, D_MODEL), F32) for n in out_rows]
    out_specs = [any_spec] * len(out_rows)
    if emit_norm:
        out_shape.append(jax.ShapeDtypeStruct((rows, D_MODEL), BF16))
        out_specs.append(any_spec)
    if convert_next:
        stacked, (li, si) = next_w
        in_specs += [
            pl.BlockSpec((None, None, D_MODEL // n_i, TF), lambda i, j: (li, si, i, j)),
            pl.BlockSpec((None, None, D_MODEL // n_i, TF), lambda i, j: (li, si, i, j)),
            pl.BlockSpec((None, None, TF, D_MODEL // n_i), lambda i, j: (li, si, j, i)),
        ]
        args += list(stacked)
        out_shape += [jax.ShapeDtypeStruct(w.shape[2:], BF16) for w in stacked]
        out_specs += [up_piece, up_piece, down_piece]
    scratch = [pltpu.VMEM((2, TM, D_MODEL), F32), pltpu.VMEM((TM, D_MODEL), BF16),
               pltpu.SemaphoreType.DMA((2,)), pltpu.SemaphoreType.DMA((2,))]
    if emit_norm:
        scratch += [pltpu.VMEM((TM, D_MODEL), BF16), pltpu.SemaphoreType.DMA(())]

    res = pl.pallas_call(
        functools.partial(_ffn_kernel, in_rows=in_rows, out_rows=tuple(out_rows),
                          convert_next=convert_next, emit_norm=emit_norm, final_norm=final_norm),
        out_shape=out_shape,
        grid=(n_i, n_j),
        in_specs=in_specs,
        out_specs=out_specs,
        scratch_shapes=scratch,
        compiler_params=_params(("arbitrary", "arbitrary")),
        name="ffn",
    )(*args)
    res = list(res)
    outs = [res.pop(0) for _ in out_rows]
    xn = res.pop(0) if emit_norm else None
    return outs, xn, (res if convert_next else None)


def _proj_kernel(a_ref, w_ref, b_ref, o_ref, wb_ref, *, gelu):
    @pl.when(pl.program_id(1) == 0)
    def _():
        wb_ref[...] = w_ref[...].astype(BF16)

    y = jnp.dot(a_ref[...], wb_ref[...], preferred_element_type=F32)
    if gelu:
        y = y + b_ref[...]
        y = 0.5 * y * (1.0 + lax.erf(y * (2.0 ** -0.5)))
    o_ref[...] = y


def _proj(a, w, b, gelu):
    rows, k = a.shape
    n = w.shape[1]
    return pl.pallas_call(
        functools.partial(_proj_kernel, gelu=gelu),
        out_shape=jax.ShapeDtypeStruct((rows, n), F32),
        grid=(n // TN, rows // TM),
        in_specs=[
            pl.BlockSpec((TM, k), lambda j, i: (i, 0)),
            pl.BlockSpec((k, TN), lambda j, i: (0, j)),
            pl.BlockSpec((1, TN), lambda j, i: (0, j)),
        ],
        out_specs=pl.BlockSpec((TM, TN), lambda j, i: (i, j)),
        scratch_shapes=[pltpu.VMEM((k, TN), BF16)],
        compiler_params=_params(("arbitrary", "arbitrary")),
        name="proj_gelu" if gelu else "proj",
    )(a, w, b)


def _proj_residual_kernel(h_ref, w_ref, x_ref, o_ref, wb_ref):
    @pl.when(pl.program_id(1) == 0)
    def _():
        wb_ref[...] = w_ref[...].astype(BF16)

    o_ref[...] = x_ref[...] + jnp.dot(h_ref[...], wb_ref[...], preferred_element_type=F32)


def _proj_residual(h, w, x):
    rows, k = h.shape
    return pl.pallas_call(
        _proj_residual_kernel,
        out_shape=jax.ShapeDtypeStruct((rows, D_MODEL), F32),
        grid=(D_MODEL // TN, rows // TM),
        in_specs=[
            pl.BlockSpec((TM, k), lambda j, i: (i, 0)),
            pl.BlockSpec((k, TN), lambda j, i: (0, j)),
            pl.BlockSpec((TM, TN), lambda j, i: (i, j)),
        ],
        out_specs=pl.BlockSpec((TM, TN), lambda j, i: (i, j)),
        scratch_shapes=[pltpu.VMEM((k, TN), BF16)],
        compiler_params=_params(("arbitrary", "arbitrary")),
        name="proj_residual",
    )(h, w, x)


def _conv_a(abuf, sbuf, wa_ref, n_rows):
    shift_rows = A_HALO + n_rows - SUBLANES
    for s in range(1, SUBLANES):
        sbuf[s - 1, 0:shift_rows, :] = abuf[s:s + shift_rows, :]

    a0 = A_HALO - (K_A - 1)

    def tap(k, r0, cs):
        q, s = divmod(a0 + k, SUBLANES)
        src = abuf if s == 0 else sbuf.at[s - 1]
        return wa_ref[k:k + 1, cs] * src[SUBLANES * q + r0:SUBLANES * q + r0 + CONV_RB, cs]

    for c in range(MIX_C // LANES):
        cs = slice(c * LANES, (c + 1) * LANES)
        for r0 in range(0, n_rows, CONV_RB):
            acc = tap(0, r0, cs)
            for k in range(1, K_A):
                acc = acc + tap(k, r0, cs)
            yield slice(r0, r0 + CONV_RB), cs, acc


def _load_bf16_columns(w_hbm, col_starts, width, dsts, stage, sem):
    def chunk(c):
        return pltpu.make_async_copy(w_hbm.at[:, pl.ds(col_starts[c], width)], stage.at[c % 2],
                                     sem.at[c % 2])
    chunk(0).start()
    for c in range(len(col_starts)):
        if c + 1 < len(col_starts):
            chunk(c + 1).start()
        chunk(c).wait()
        dsts[c][...] = stage[c % 2].astype(BF16)


def _mixer_in_kernel(xn_ref, w_hbm, wa_ref, ba_ref, wb_ref, sa_ref, sb_ref,
                     ac_ref, hb_ref, nap_ref, nbp_ref, nas_ref, nbs_ref,
                     wbf, abuf, sbuf, bbuf, stage, sem, *, n_prompt_tiles, tiles_per_seq):
    c_tile = pl.program_id(0)
    r = pl.program_id(1)

    @pl.when(r == 0)
    def _():
        col = pl.multiple_of(c_tile * MIX_C, MIX_C)
        _load_bf16_columns(w_hbm, [g * D_A + col for g in range(5)], MIX_C,
                           [wbf.at[:, pl.ds(g * MIX_C, MIX_C)] for g in range(5)], stage, sem)

    def project(xn):
        p = jnp.dot(xn, wbf[...], preferred_element_type=F32)
        return [p[:, g * MIX_C:(g + 1) * MIX_C] for g in range(5)]

    @pl.when(r < n_prompt_tiles)
    def _():
        @pl.when(lax.rem(r, tiles_per_seq) == 0)
        def _():
            abuf[0:A_HALO, :] = jnp.zeros((A_HALO, MIX_C), F32)
            bbuf[0:B_HALO, :] = jnp.zeros((B_HALO, MIX_C), F32)

        pa, ga, bg, cg, hb = project(xn_ref[...])
        abuf[A_HALO:A_HALO + MIX_T, :] = pa * jax.nn.sigmoid(ga)
        bbuf[B_HALO:B_HALO + MIX_T, :] = cg * hb

        for rs, cs, acc in _conv_a(abuf, sbuf, wa_ref, MIX_T):
            ac_ref[rs, cs] = acc + ba_ref[:, cs]

        b0 = B_HALO - (K_B - 1)
        b_c = wb_ref[0:1, :] * bbuf[b0:b0 + MIX_T, :]
        for k in range(1, K_B):
            b_c = b_c + wb_ref[k:k + 1, :] * bbuf[b0 + k:b0 + k + MIX_T, :]
        hb_ref[...] = (bg * b_c).astype(BF16)

        @pl.when(lax.rem(r, tiles_per_seq) == tiles_per_seq - 1)
        def _():
            nap_ref[0] = abuf[MIX_T + A_HALO - (K_A - 1):MIX_T + A_HALO, :]
            nbp_ref[0] = bbuf[MIX_T + b0:MIX_T + B_HALO, :]

        abuf[0:A_HALO, :] = abuf[MIX_T:MIX_T + A_HALO, :]
        bbuf[0:B_HALO, :] = bbuf[MIX_T:MIX_T + B_HALO, :]

    @pl.when(r >= n_prompt_tiles)
    def _():
        rs = pl.ds(pl.multiple_of((r - n_prompt_tiles) * SAMPLE_BS, SAMPLE_BS), SAMPLE_BS)
        pa, ga, bg, cg, hb = project(xn_ref[rs, :])
        a = pa * jax.nn.sigmoid(ga)
        acc = wa_ref[K_A - 1:K_A, :] * a
        for k in range(K_A - 1):
            acc = acc + wa_ref[k:k + 1, :] * sa_ref[:, k, :]
        ac_ref[rs, :] = acc + ba_ref[...]
        for k in range(K_A - 2):
            nas_ref[:, k, :] = sa_ref[:, k + 1, :]
        nas_ref[:, K_A - 2, :] = a

        ch = cg * hb
        b_c = wb_ref[K_B - 1:K_B, :] * ch
        for k in range(K_B - 1):
            b_c = b_c + wb_ref[k:k + 1, :] * sb_ref[:, k, :]
        hb_ref[rs, :] = (bg * b_c).astype(BF16)
        for k in range(K_B - 2):
            nbs_ref[:, k, :] = sb_ref[:, k + 1, :]
        nbs_ref[:, K_B - 2, :] = ch


def _mixer_in(xn, w_in, wa, ba, wb, sa, sb, batch, seq):
    rows = xn.shape[0]
    n_sample = sa.shape[0]
    n_prompt_tiles = batch * seq // MIX_T
    tiles_per_seq = seq // MIX_T
    assert rows == batch * seq + n_sample and n_sample <= MIX_T and seq % MIX_T == 0
    assert n_sample % SAMPLE_BS == 0
    row_blk = lambda r: jnp.minimum(r, n_prompt_tiles)
    smp_blk = lambda r: jnp.maximum(r - n_prompt_tiles, 0)
    seq_of = lambda r: jnp.minimum(r, n_prompt_tiles - 1) // tiles_per_seq
    chan = lambda k: pl.BlockSpec((k, MIX_C), lambda c, r: (0, c))
    state = lambda k: pl.BlockSpec((SAMPLE_BS, k, MIX_C), lambda c, r: (smp_blk(r), 0, c))
    new_state = lambda k: pl.BlockSpec((None, SAMPLE_BS, k, MIX_C),
                                       lambda c, r: (0, smp_blk(r), 0, c))
    prompt_state = lambda k: pl.BlockSpec((None, 1, k, MIX_C), lambda c, r: (0, seq_of(r), 0, c))
    return pl.pallas_call(
        functools.partial(_mixer_in_kernel, n_prompt_tiles=n_prompt_tiles,
                          tiles_per_seq=tiles_per_seq),
        out_shape=(jax.ShapeDtypeStruct((rows, D_A), F32),
                   jax.ShapeDtypeStruct((rows, D_B), BF16),
                   jax.ShapeDtypeStruct((1, batch, K_A - 1, D_A), F32),
                   jax.ShapeDtypeStruct((1, batch, K_B - 1, D_B), F32),
                   jax.ShapeDtypeStruct((1, n_sample, K_A - 1, D_A), F32),
                   jax.ShapeDtypeStruct((1, n_sample, K_B - 1, D_B), F32)),
        grid=(D_A // MIX_C, n_prompt_tiles + n_sample // SAMPLE_BS),
        in_specs=[pl.BlockSpec((MIX_T, D_MODEL), lambda c, r: (row_blk(r), 0)),
                  pl.BlockSpec(memory_space=pl.ANY),
                  chan(K_A), chan(1), chan(K_B), state(K_A - 1), state(K_B - 1)],
        out_specs=(pl.BlockSpec((MIX_T, MIX_C), lambda c, r: (row_blk(r), c)),
                   pl.BlockSpec((MIX_T, MIX_C), lambda c, r: (row_blk(r), c)),
                   prompt_state(K_A - 1), prompt_state(K_B - 1),
                   new_state(K_A - 1), new_state(K_B - 1)),
        scratch_shapes=[pltpu.VMEM((D_MODEL, 5 * MIX_C), BF16),
                        pltpu.VMEM((A_HALO + MIX_T, MIX_C), F32),
                        pltpu.VMEM((SUBLANES - 1, A_HALO + MIX_T - SUBLANES, MIX_C), F32),
                        pltpu.VMEM((B_HALO + MIX_T, MIX_C), F32),
                        pltpu.VMEM((2, D_MODEL, MIX_C), F32),
                        pltpu.SemaphoreType.DMA((2,))],
        compiler_params=_params(("arbitrary", "arbitrary")),
        name="mixer_in",
    )(xn, w_in, wa, ba, wb, sa, sb)


def _mixer_out_kernel(ac_ref, hb_ref, lg_ref, lb_ref, w_hbm, x_ref, o_ref, wb_ref, stage, sem):
    j = pl.program_id(0)

    @pl.when(pl.program_id(1) == 0)
    def _():
        col = pl.multiple_of(j * TN, TN)
        offs = range(0, TN, W_STAGE)
        _load_bf16_columns(w_hbm, [col + o for o in offs], W_STAGE,
                           [wb_ref.at[:, pl.ds(o, W_STAGE)] for o in offs], stage, sem)

    a_ln = _layernorm_rows(ac_ref[...], lg_ref[...], lb_ref[...])
    h_a = (a_ln * jax.nn.sigmoid(a_ln)).astype(BF16)
    y = jnp.dot(h_a, wb_ref[0:D_A, :], preferred_element_type=F32)
    y = y + jnp.dot(hb_ref[...], wb_ref[D_A:D_A + D_B, :], preferred_element_type=F32)
    o_ref[...] = x_ref[...] + y


def _mixer_out(ac, hb, lg, lb, w, x):
    rows = x.shape[0]
    return pl.pallas_call(
        _mixer_out_kernel,
        out_shape=jax.ShapeDtypeStruct((rows, D_MODEL), F32),
        grid=(D_MODEL // TN, rows // TM),
        in_specs=[
            pl.BlockSpec((TM, D_A), lambda j, i: (i, 0)),
            pl.BlockSpec((TM, D_B), lambda j, i: (i, 0)),
            pl.BlockSpec((1, D_A), lambda j, i: (0, 0)),
            pl.BlockSpec((1, D_A), lambda j, i: (0, 0)),
            pl.BlockSpec(memory_space=pl.ANY),
            pl.BlockSpec((TM, TN), lambda j, i: (i, j)),
        ],
        out_specs=pl.BlockSpec((TM, TN), lambda j, i: (i, j)),
        scratch_shapes=[pltpu.VMEM((D_A + D_B, TN), BF16),
                        pltpu.VMEM((2, D_A + D_B, W_STAGE), F32),
                        pltpu.SemaphoreType.DMA((2,))],
        compiler_params=_params(("arbitrary", "arbitrary")),
        name="mixer_out",
    )(ac, hb, lg, lb, w, x)


def _gate_kernel(z_ref, lg_ref, lb_ref, ws_ref, bs_ref, o_ref, v_ref, *, n_prompt_chunks):
    i = pl.program_id(0)
    row = lax.broadcasted_iota(jnp.int32, (CHUNK, CHUNK), 0)
    col = lax.broadcasted_iota(jnp.int32, (CHUNK, CHUNK), 1)
    for c in range(GATE_CHUNKS):
        rs = slice(c * CHUNK, (c + 1) * CHUNK)
        is_sample = i * GATE_CHUNKS + c >= n_prompt_chunks
        sel = jnp.where(is_sample, 1, 0)
        vn = _layernorm_rows(z_ref[rs, D_C:2 * D_C], lg_ref[...], lb_ref[...])
        vb = vn.astype(BF16)
        keep = (col <= row) & (col >= jnp.where(is_sample, row, 0))
        for h in range(H_C):
            hs = slice(h * DH_C, (h + 1) * DH_C)
            w = jnp.where(keep, ws_ref[sel, h], 0.0).astype(BF16)
            s = jnp.dot(w, vb[:, hs], preferred_element_type=F32) + bs_ref[sel][:, h:h + 1]
            o_ref[rs, hs] = (z_ref[rs, hs] * s).astype(BF16)

        if c == GATE_CHUNKS - 1:
            @pl.when(i == pl.num_programs(0) - 1)
            def _():
                v_ref[...] = vn


def _gate(z, lg, lb, ws_all, bs_all, n_prompt_chunks):
    rows = z.shape[0]
    tile = GATE_CHUNKS * CHUNK
    assert rows % tile == 0 and rows // CHUNK == n_prompt_chunks + 1
    return pl.pallas_call(
        functools.partial(_gate_kernel, n_prompt_chunks=n_prompt_chunks),
        out_shape=(jax.ShapeDtypeStruct((rows, D_C), BF16),
                   jax.ShapeDtypeStruct((CHUNK, D_C), F32)),
        grid=(rows // tile,),
        in_specs=[
            pl.BlockSpec((tile, 2 * D_C), lambda i: (i, 0)),
            pl.BlockSpec((1, D_C), lambda i: (0, 0)),
            pl.BlockSpec((1, D_C), lambda i: (0, 0)),
            pl.BlockSpec((2, H_C, CHUNK, CHUNK), lambda i: (0, 0, 0, 0)),
            pl.BlockSpec((2, CHUNK, H_C), lambda i: (0, 0, 0)),
        ],
        out_specs=(pl.BlockSpec((tile, D_C), lambda i: (i, 0)),
                   pl.BlockSpec((CHUNK, D_C), lambda i: (0, 0))),
        compiler_params=_params(("arbitrary",)),
        name="gate",
    )(z, lg, lb, ws_all, bs_all)


def kernel(x_prompt, x_sample, state_conv_a, state_conv_b, norm_g, ffn_w1, ffn_w3, ffn_w2,
           ab_w_in, a_conv_w, a_conv_b, a_ln_g, a_ln_b, b_conv_w, ab_w_out,
           c_w_in, c_b_in, c_ln_g, c_ln_b, c_w_s, c_b_s, c_w_out, final_g):
    batch, seq, _ = x_prompt.shape
    nb, dec_seq, _ = x_sample.shape
    depth = norm_g.shape[0]
    n_prompt = batch * seq
    rows = n_prompt + nb
    assert dec_seq == 1 and nb == CHUNK and n_prompt % CHUNK == 0

    row = lambda v: v.reshape(1, -1)
    ffn_stacked = (ffn_w1, ffn_w3, ffn_w2)
    w_bf16 = [w[0, 0].astype(BF16) for w in ffn_stacked]

    xs = [x_prompt.reshape(n_prompt, D_MODEL), x_sample.reshape(nb, D_MODEL)]
    new_a_p, new_b_p, new_a_s, new_b_s, new_v_s = [], [], [], [], []
    for i in range(depth):
        j = i // 2
        (x,), xn, w_bf16 = _ffn(xs, row(norm_g[i, 0]), *w_bf16, row(norm_g[i, 1]), (rows,),
                                next_w=(ffn_stacked, (i, 1)), emit_norm=True)
        if i % 2 == 0:
            ac, hb, sa_p, sb_p, sa_s, sb_s = _mixer_in(
                xn, ab_w_in[j], a_conv_w[j], row(a_conv_b[j]), b_conv_w[j],
                state_conv_a[j], state_conv_b[j], batch, seq)
            new_a_p.append(sa_p)
            new_b_p.append(sb_p)
            new_a_s.append(sa_s)
            new_b_s.append(sb_s)
            x = _mixer_out(ac, hb, row(a_ln_g[j]), row(a_ln_b[j]), ab_w_out[j], x)
        else:
            z = _proj(xn, c_w_in[j], row(c_b_in[j]), True)
            w_s = c_w_s[j][:, :CHUNK, :CHUNK]
            ws_all = jnp.stack([w_s, jnp.broadcast_to(w_s[:, :1, :1], w_s.shape)])
            b_s = c_b_s[j][:, :CHUNK].T
            bs_all = jnp.stack([b_s, jnp.broadcast_to(b_s[:1], b_s.shape)])
            h, v_s = _gate(z, row(c_ln_g[j]), row(c_ln_b[j]), ws_all, bs_all, n_prompt // CHUNK)
            new_v_s.append(v_s.reshape(nb, 1, D_C))
            x = _proj_residual(h, c_w_out[j], x)
        last = i == depth - 1
        xs, _, w_bf16 = _ffn([x], row(norm_g[i, 2]), *w_bf16, row(final_g),
                             (n_prompt, nb) if last else (rows,),
                             next_w=None if last else (ffn_stacked, (i + 1, 0)), final_norm=last)

    y_prompt, y_sample = xs
    return (y_prompt.reshape(batch, seq, D_MODEL), y_sample.reshape(nb, 1, D_MODEL),
            jnp.concatenate(new_a_p), jnp.concatenate(new_b_p),
            jnp.concatenate(new_a_s), jnp.concatenate(new_b_s),
            jnp.stack(new_v_s))
```

```python
import functools

import jax
import jax.numpy as jnp
from jax import lax
from jax.experimental import pallas as pl
from jax.experimental.pallas import tpu as pltpu

F32 = jnp.float32
BF16 = jnp.bfloat16

D_MODEL = 2048
D_FF = 5632
D_A = 1024
D_B = 1024
K_A = 31
K_B = 3
D_C = 2048
H_C = 8
DH_C = D_C // H_C
CHUNK = 128
EPS = 1e-6

VMEM_LIMIT_BYTES = 56 * 1024 * 1024
LANES = 128
SUBLANES = 8

TM = 1040
NORM_ROWS = 208
TF = 512
TN = 1024
W_STAGE = 256
MIX_T = 1024
MIX_C = 256
SAMPLE_BS = 64
CONV_RB = 64
A_HALO = 32
B_HALO = 8
GATE_CHUNKS = 5


def _params(sem):
    return pltpu.CompilerParams(dimension_semantics=sem, vmem_limit_bytes=VMEM_LIMIT_BYTES)


def _rms_rows(x, g):
    ms = jnp.mean(x * x, axis=-1, keepdims=True)
    return x * lax.rsqrt(ms + EPS) * g


def _layernorm_rows(x, g, b):
    mu = jnp.mean(x, axis=-1, keepdims=True)
    xc = x - mu
    var = jnp.mean(xc * xc, axis=-1, keepdims=True)
    return xc * lax.rsqrt(var + EPS) * g + b


def _norm_chunks(src_ref, g_ref, dst_ref, dtype):
    def body(c, carry):
        r = pl.ds(pl.multiple_of(c * NORM_ROWS, NORM_ROWS), NORM_ROWS)
        dst_ref[r, :] = _rms_rows(src_ref[r, :], g_ref[...]).astype(dtype)
        return carry
    lax.fori_loop(0, TM // NORM_ROWS, body, 0)


def _tile_pieces(tile, seg_rows):
    lo, hi = tile * TM, (tile + 1) * TM
    pieces, base = [], 0
    for k, n in enumerate(seg_rows):
        a, b = max(lo, base), min(hi, base + n)
        if a < b:
            pieces.append((k, a - base, a - lo, b - a))
        base += n
    return pieces


def _ffn_kernel(*refs, in_rows, out_rows, convert_next, emit_norm, final_norm):
    n_in, n_out = len(in_rows), len(out_rows)
    it = iter(refs)
    x_srcs = [next(it) for _ in range(n_in)]
    g_ref, w1_ref, w3_ref, w2_ref, post_g_ref = (next(it) for _ in range(5))
    next_w = [next(it) for _ in range(3)] if convert_next else []
    o_dsts = [next(it) for _ in range(n_out)]
    xn_out = next(it) if emit_norm else None
    next_w_out = [next(it) for _ in range(3)] if convert_next else []
    acc_ref, xn_ref, load_sem, store_sem = (next(it) for _ in range(4))
    xn_next_ref, xn_sem = (next(it), next(it)) if emit_norm else (None, None)

    i = pl.program_id(0)
    j = pl.program_id(1)
    n_i = pl.num_programs(0)
    n_j = pl.num_programs(1)
    n_tiles = sum(in_rows) // TM

    def load_copies(t):
        return [pltpu.make_async_copy(x_srcs[k].at[pl.ds(r0, n)],
                                      acc_ref.at[t % 2, pl.ds(d0, n)], load_sem.at[t % 2])
                for k, r0, d0, n in _tile_pieces(t, in_rows)]

    def store_copies(t):
        return [pltpu.make_async_copy(acc_ref.at[t % 2, pl.ds(d0, n)],
                                      o_dsts[k].at[pl.ds(r0, n)], store_sem.at[t % 2])
                for k, r0, d0, n in _tile_pieces(t, out_rows)]

    def xn_copy(t):
        return pltpu.make_async_copy(xn_next_ref, xn_out.at[pl.ds(t * TM, TM)], xn_sem)

    def on_tile(tile, fn):
        for t in range(n_tiles):
            @pl.when(tile == t)
            def _(t=t):
                fn(t)

    acc = acc_ref.at[lax.rem(i, 2)]

    @pl.when(j == 0)
    def _():
        @pl.when(i == 0)
        def _():
            for cp in load_copies(0):
                cp.start()
        on_tile(i, lambda t: [cp.wait() for cp in load_copies(t)])
        _norm_chunks(acc, g_ref, xn_ref, BF16)

    @pl.when(j == 2)
    def _():
        @pl.when(i > 0)
        def _():
            on_tile(i - 1, lambda t: [cp.wait() for cp in store_copies(t)])
            if emit_norm:
                on_tile(i - 1, lambda t: xn_copy(t).wait())

        @pl.when(i + 1 < n_i)
        def _():
            on_tile(i + 1, lambda t: [cp.start() for cp in load_copies(t)])

    xn = xn_ref[...]
    h1 = jnp.dot(xn, w1_ref[...], preferred_element_type=F32)
    h3 = jnp.dot(xn, w3_ref[...], preferred_element_type=F32)
    h = (0.5 * (h1 * jax.nn.sigmoid(h1)) * h3).astype(BF16)
    for n in range(D_MODEL // TF):
        cs = slice(n * TF, (n + 1) * TF)
        acc[:, cs] += jnp.dot(h, w2_ref[:, cs], preferred_element_type=F32)

    if convert_next:
        for src, dst in zip(next_w, next_w_out):
            dst[...] = src[...].astype(BF16)

    @pl.when(j == n_j - 1)
    def _():
        if final_norm:
            _norm_chunks(acc, post_g_ref, acc, F32)
        on_tile(i, lambda t: [cp.start() for cp in store_copies(t)])
        if emit_norm:
            _norm_chunks(acc, post_g_ref, xn_next_ref, BF16)
            on_tile(i, lambda t: xn_copy(t).start())

        @pl.when(i == n_i - 1)
        def _():
            on_tile(i, lambda t: [cp.wait() for cp in store_copies(t)])
            if emit_norm:
                on_tile(i, lambda t: xn_copy(t).wait())


def _ffn(xs, g, w1, w3, w2, post_g, out_rows, next_w=None, emit_norm=False, final_norm=False):
    in_rows = tuple(a.shape[0] for a in xs)
    rows = sum(in_rows)
    assert rows == sum(out_rows) and rows % TM == 0 and D_FF % TF == 0
    n_i, n_j = rows // TM, D_FF // TF
    assert n_j > 2 and D_MODEL % n_i == 0 and TF % n_i == 0
    convert_next = next_w is not None
    any_spec = pl.BlockSpec(memory_space=pl.ANY)
    row_spec = pl.BlockSpec((1, D_MODEL), lambda i, j: (0, 0))
    up_piece = pl.BlockSpec((D_MODEL // n_i, TF), lambda i, j: (i, j))
    down_piece = pl.BlockSpec((TF, D_MODEL // n_i), lambda i, j: (j, i))

    in_specs = [any_spec] * len(xs) + [
        row_spec,
        pl.BlockSpec((D_MODEL, TF), lambda i, j: (0, j)),
        pl.BlockSpec((D_MODEL, TF), lambda i, j: (0, j)),
        pl.BlockSpec((TF, D_MODEL), lambda i, j: (j, 0)),
        row_spec,
    ]
    args = list(xs) + [g, w1, w3, w2, post_g]
    out_shape = [jax.ShapeDtypeStruct((n, D_MODEL), F32) for n in out_rows]
    out_specs = [any_spec] * len(out_rows)
    if emit_norm:
        out_shape.append(jax.ShapeDtypeStruct((rows, D_MODEL), BF16))
        out_specs.append(any_spec)
    if convert_next:
        stacked, (li, si) = next_w
        in_specs += [
            pl.BlockSpec((None, None, D_MODEL // n_i, TF), lambda i, j: (li, si, i, j)),
            pl.BlockSpec((None, None, D_MODEL // n_i, TF), lambda i, j: (li, si, i, j)),
            pl.BlockSpec((None, None, TF, D_MODEL // n_i), lambda i, j: (li, si, j, i)),
        ]
        args += list(stacked)
        out_shape += [jax.ShapeDtypeStruct(w.shape[2:], BF16) for w in stacked]
        out_specs += [up_piece, up_piece, down_piece]
    scratch = [pltpu.VMEM((2, TM, D_MODEL), F32), pltpu.VMEM((TM, D_MODEL), BF16),
               pltpu.SemaphoreType.DMA((2,)), pltpu.SemaphoreType.DMA((2,))]
    if emit_norm:
        scratch += [pltpu.VMEM((TM, D_MODEL), BF16), pltpu.SemaphoreType.DMA(())]

    res = pl.pallas_call(
        functools.partial(_ffn_kernel, in_rows=in_rows, out_rows=tuple(out_rows),
                          convert_next=convert_next, emit_norm=emit_norm, final_norm=final_norm),
        out_shape=out_shape,
        grid=(n_i, n_j),
        in_specs=in_specs,
        out_specs=out_specs,
        scratch_shapes=scratch,
        compiler_params=_params(("arbitrary", "arbitrary")),
        name="ffn",
    )(*args)
    res = list(res)
    outs = [res.pop(0) for _ in out_rows]
    xn = res.pop(0) if emit_norm else None
    return outs, xn, (res if convert_next else None)


def _proj_kernel(a_ref, w_ref, b_ref, o_ref, wb_ref, *, gelu):
    @pl.when(pl.program_id(1) == 0)
    def _():
        wb_ref[...] = w_ref[...].astype(BF16)

    y = jnp.dot(a_ref[...], wb_ref[...], preferred_element_type=F32)
    if gelu:
        y = y + b_ref[...]
        y = 0.5 * y * (1.0 + lax.erf(y * (2.0 ** -0.5)))
    o_ref[...] = y


def _proj(a, w, b, gelu):
    rows, k = a.shape
    n = w.shape[1]
    return pl.pallas_call(
        functools.partial(_proj_kernel, gelu=gelu),
        out_shape=jax.ShapeDtypeStruct((rows, n), F32),
        grid=(n // TN, rows // TM),
        in_specs=[
            pl.BlockSpec((TM, k), lambda j, i: (i, 0)),
            pl.BlockSpec((k, TN), lambda j, i: (0, j)),
            pl.BlockSpec((1, TN), lambda j, i: (0, j)),
        ],
        out_specs=pl.BlockSpec((TM, TN), lambda j, i: (i, j)),
        scratch_shapes=[pltpu.VMEM((k, TN), BF16)],
        compiler_params=_params(("arbitrary", "arbitrary")),
        name="proj_gelu" if gelu else "proj",
    )(a, w, b)


def _proj_residual_kernel(h_ref, w_ref, x_ref, o_ref, wb_ref):
    @pl.when(pl.program_id(1) == 0)
    def _():
        wb_ref[...] = w_ref[...].astype(BF16)

    o_ref[...] = x_ref[...] + jnp.dot(h_ref[...], wb_ref[...], preferred_element_type=F32)


def _proj_residual(h, w, x):
    rows, k = h.shape
    return pl.pallas_call(
        _proj_residual_kernel,
        out_shape=jax.ShapeDtypeStruct((rows, D_MODEL), F32),
        grid=(D_MODEL // TN, rows // TM),
        in_specs=[
            pl.BlockSpec((TM, k), lambda j, i: (i, 0)),
            pl.BlockSpec((k, TN), lambda j, i: (0, j)),
            pl.BlockSpec((TM, TN), lambda j, i: (i, j)),
        ],
        out_specs=pl.BlockSpec((TM, TN), lambda j, i: (i, j)),
        scratch_shapes=[pltpu.VMEM((k, TN), BF16)],
        compiler_params=_params(("arbitrary", "arbitrary")),
        name="proj_residual",
    )(h, w, x)


def _conv_a(abuf, sbuf, wa_ref, n_rows):
    shift_rows = A_HALO + n_rows - SUBLANES
    for s in range(1, SUBLANES):
        sbuf[s - 1, 0:shift_rows, :] = abuf[s:s + shift_rows, :]

    a0 = A_HALO - (K_A - 1)

    def tap(k, r0, cs):
        q, s = divmod(a0 + k, SUBLANES)
        src = abuf if s == 0 else sbuf.at[s - 1]
        return wa_ref[k:k + 1, cs] * src[SUBLANES * q + r0:SUBLANES * q + r0 + CONV_RB, cs]

    for c in range(MIX_C // LANES):
        cs = slice(c * LANES, (c + 1) * LANES)
        for r0 in range(0, n_rows, CONV_RB):
            acc = tap(0, r0, cs)
            for k in range(1, K_A):
                acc = acc + tap(k, r0, cs)
            yield slice(r0, r0 + CONV_RB), cs, acc


def _load_bf16_columns(w_hbm, col_starts, width, dsts, stage, sem):
    def chunk(c):
        return pltpu.make_async_copy(w_hbm.at[:, pl.ds(col_starts[c], width)], stage.at[c % 2],
                                     sem.at[c % 2])
    chunk(0).start()
    for c in range(len(col_starts)):
        if c + 1 < len(col_starts):
            chunk(c + 1).start()
        chunk(c).wait()
        dsts[c][...] = stage[c % 2].astype(BF16)


def _mixer_in_kernel(xn_ref, w_hbm, wa_ref, ba_ref, wb_ref, sa_ref, sb_ref,
                     ac_ref, hb_ref, nap_ref, nbp_ref, nas_ref, nbs_ref,
                     wbf, abuf, sbuf, bbuf, stage, sem, *, n_prompt_tiles, tiles_per_seq):
    c_tile = pl.program_id(0)
    r = pl.program_id(1)
    n_groups = 5
    wcur = wbf.at[lax.rem(c_tile, 2)]
    wnext = wbf.at[1 - lax.rem(c_tile, 2)]

    @pl.when((c_tile == 0) & (r == 0))
    def _():
        _load_bf16_columns(w_hbm, [g * D_A for g in range(n_groups)], MIX_C,
                           [wcur.at[:, pl.ds(g * MIX_C, MIX_C)] for g in range(n_groups)],
                           stage, sem)

    @pl.when(c_tile + 1 < pl.num_programs(0))
    def _():
        col = pl.multiple_of((c_tile + 1) * MIX_C, MIX_C)

        def fetch(g):
            return pltpu.make_async_copy(w_hbm.at[:, pl.ds(g * D_A + col, MIX_C)],
                                         stage.at[g % 2], sem.at[g % 2])
        for g in range(n_groups):
            @pl.when(r == g + 2)
            def _(g=g):
                fetch(g).wait()
                wnext[:, g * MIX_C:(g + 1) * MIX_C] = stage[g % 2].astype(BF16)

            @pl.when(r == g + 1)
            def _(g=g):
                fetch(g).start()

    def project(xn):
        p = jnp.dot(xn, wcur[...], preferred_element_type=F32)
        return [p[:, g * MIX_C:(g + 1) * MIX_C] for g in range(n_groups)]

    @pl.when(r < n_prompt_tiles)
    def _():
        @pl.when(lax.rem(r, tiles_per_seq) == 0)
        def _():
            abuf[0:A_HALO, :] = jnp.zeros((A_HALO, MIX_C), F32)
            bbuf[0:B_HALO, :] = jnp.zeros((B_HALO, MIX_C), F32)

        pa, ga, bg, cg, hb = project(xn_ref[...])
        abuf[A_HALO:A_HALO + MIX_T, :] = pa * jax.nn.sigmoid(ga)
        bbuf[B_HALO:B_HALO + MIX_T, :] = cg * hb

        for rs, cs, acc in _conv_a(abuf, sbuf, wa_ref, MIX_T):
            ac_ref[rs, cs] = acc + ba_ref[:, cs]

        b0 = B_HALO - (K_B - 1)
        b_c = wb_ref[0:1, :] * bbuf[b0:b0 + MIX_T, :]
        for k in range(1, K_B):
            b_c = b_c + wb_ref[k:k + 1, :] * bbuf[b0 + k:b0 + k + MIX_T, :]
        hb_ref[...] = (bg * b_c).astype(BF16)

        @pl.when(lax.rem(r, tiles_per_seq) == tiles_per_seq - 1)
        def _():
            nap_ref[0] = abuf[MIX_T + A_HALO - (K_A - 1):MIX_T + A_HALO, :]
            nbp_ref[0] = bbuf[MIX_T + b0:MIX_T + B_HALO, :]

        abuf[0:A_HALO, :] = abuf[MIX_T:MIX_T + A_HALO, :]
        bbuf[0:B_HALO, :] = bbuf[MIX_T:MIX_T + B_HALO, :]

    @pl.when(r >= n_prompt_tiles)
    def _():
        rs = pl.ds(pl.multiple_of((r - n_prompt_tiles) * SAMPLE_BS, SAMPLE_BS), SAMPLE_BS)
        pa, ga, bg, cg, hb = project(xn_ref[rs, :])
        a = pa * jax.nn.sigmoid(ga)
        acc = wa_ref[K_A - 1:K_A, :] * a
        for k in range(K_A - 1):
            acc = acc + wa_ref[k:k + 1, :] * sa_ref[:, k, :]
        ac_ref[rs, :] = acc + ba_ref[...]
        for k in range(K_A - 2):
            nas_ref[:, k, :] = sa_ref[:, k + 1, :]
        nas_ref[:, K_A - 2, :] = a

        ch = cg * hb
        b_c = wb_ref[K_B - 1:K_B, :] * ch
        for k in range(K_B - 1):
            b_c = b_c + wb_ref[k:k + 1, :] * sb_ref[:, k, :]
        hb_ref[rs, :] = (bg * b_c).astype(BF16)
        for k in range(K_B - 2):
            nbs_ref[:, k, :] = sb_ref[:, k + 1, :]
        nbs_ref[:, K_B - 2, :] = ch


def _mixer_in(xn, w_in, wa, ba, wb, sa, sb, layer, batch, seq):
    rows = xn.shape[0]
    n_sample = sa.shape[1]
    n_prompt_tiles = batch * seq // MIX_T
    tiles_per_seq = seq // MIX_T
    assert rows == batch * seq + n_sample and n_sample <= MIX_T and seq % MIX_T == 0
    assert n_sample % SAMPLE_BS == 0
    assert n_prompt_tiles + n_sample // SAMPLE_BS >= 7
    row_blk = lambda r: jnp.minimum(r, n_prompt_tiles)
    smp_blk = lambda r: jnp.maximum(r - n_prompt_tiles, 0)
    seq_of = lambda r: jnp.minimum(r, n_prompt_tiles - 1) // tiles_per_seq
    chan = lambda k: pl.BlockSpec((k, MIX_C), lambda c, r: (0, c))
    state = lambda k: pl.BlockSpec((None, SAMPLE_BS, k, MIX_C),
                                   lambda c, r: (layer, smp_blk(r), 0, c))
    new_state = lambda k: pl.BlockSpec((None, SAMPLE_BS, k, MIX_C),
                                       lambda c, r: (0, smp_blk(r), 0, c))
    prompt_state = lambda k: pl.BlockSpec((None, 1, k, MIX_C), lambda c, r: (0, seq_of(r), 0, c))
    return pl.pallas_call(
        functools.partial(_mixer_in_kernel, n_prompt_tiles=n_prompt_tiles,
                          tiles_per_seq=tiles_per_seq),
        out_shape=(jax.ShapeDtypeStruct((rows, D_A), F32),
                   jax.ShapeDtypeStruct((rows, D_B), BF16),
                   jax.ShapeDtypeStruct((1, batch, K_A - 1, D_A), F32),
                   jax.ShapeDtypeStruct((1, batch, K_B - 1, D_B), F32),
                   jax.ShapeDtypeStruct((1, n_sample, K_A - 1, D_A), F32),
                   jax.ShapeDtypeStruct((1, n_sample, K_B - 1, D_B), F32)),
        grid=(D_A // MIX_C, n_prompt_tiles + n_sample // SAMPLE_BS),
        in_specs=[pl.BlockSpec((MIX_T, D_MODEL), lambda c, r: (row_blk(r), 0)),
                  pl.BlockSpec(memory_space=pl.ANY),
                  chan(K_A), chan(1), chan(K_B), state(K_A - 1), state(K_B - 1)],
        out_specs=(pl.BlockSpec((MIX_T, MIX_C), lambda c, r: (row_blk(r), c)),
                   pl.BlockSpec((MIX_T, MIX_C), lambda c, r: (row_blk(r), c)),
                   prompt_state(K_A - 1), prompt_state(K_B - 1),
                   new_state(K_A - 1), new_state(K_B - 1)),
        scratch_shapes=[pltpu.VMEM((2, D_MODEL, 5 * MIX_C), BF16),
                        pltpu.VMEM((A_HALO + MIX_T, MIX_C), F32),
                        pltpu.VMEM((SUBLANES - 1, A_HALO + MIX_T - SUBLANES, MIX_C), F32),
                        pltpu.VMEM((B_HALO + MIX_T, MIX_C), F32),
                        pltpu.VMEM((2, D_MODEL, MIX_C), F32),
                        pltpu.SemaphoreType.DMA((2,))],
        compiler_params=_params(("arbitrary", "arbitrary")),
        name="mixer_in",
    )(xn, w_in, wa, ba, wb, sa, sb)


def _mixer_out_kernel(ac_ref, hb_ref, lg_ref, lb_ref, w_hbm, x_ref, o_ref, wb_ref, stage, sem):
    j = pl.program_id(0)

    @pl.when(pl.program_id(1) == 0)
    def _():
        col = pl.multiple_of(j * TN, TN)
        offs = range(0, TN, W_STAGE)
        _load_bf16_columns(w_hbm, [col + o for o in offs], W_STAGE,
                           [wb_ref.at[:, pl.ds(o, W_STAGE)] for o in offs], stage, sem)

    a_ln = _layernorm_rows(ac_ref[...], lg_ref[...], lb_ref[...])
    h_a = (a_ln * jax.nn.sigmoid(a_ln)).astype(BF16)
    y = jnp.dot(h_a, wb_ref[0:D_A, :], preferred_element_type=F32)
    y = y + jnp.dot(hb_ref[...], wb_ref[D_A:D_A + D_B, :], preferred_element_type=F32)
    o_ref[...] = x_ref[...] + y


def _mixer_out(ac, hb, lg, lb, w, x):
    rows = x.shape[0]
    return pl.pallas_call(
        _mixer_out_kernel,
        out_shape=jax.ShapeDtypeStruct((rows, D_MODEL), F32),
        grid=(D_MODEL // TN, rows // TM),
        in_specs=[
            pl.BlockSpec((TM, D_A), lambda j, i: (i, 0)),
            pl.BlockSpec((TM, D_B), lambda j, i: (i, 0)),
            pl.BlockSpec((1, D_A), lambda j, i: (0, 0)),
            pl.BlockSpec((1, D_A), lambda j, i: (0, 0)),
            pl.BlockSpec(memory_space=pl.ANY),
            pl.BlockSpec((TM, TN), lambda j, i: (i, j)),
        ],
        out_specs=pl.BlockSpec((TM, TN), lambda j, i: (i, j)),
        scratch_shapes=[pltpu.VMEM((D_A + D_B, TN), BF16),
                        pltpu.VMEM((2, D_A + D_B, W_STAGE), F32),
                        pltpu.SemaphoreType.DMA((2,))],
        compiler_params=_params(("arbitrary", "arbitrary")),
        name="mixer_out",
    )(ac, hb, lg, lb, w, x)


def _gate_kernel(z_ref, lg_ref, lb_ref, ws_ref, bs_ref, o_ref, v_ref, *, n_prompt_chunks):
    i = pl.program_id(0)
    row = lax.broadcasted_iota(jnp.int32, (CHUNK, CHUNK), 0)
    col = lax.broadcasted_iota(jnp.int32, (CHUNK, CHUNK), 1)
    for c in range(GATE_CHUNKS):
        rs = slice(c * CHUNK, (c + 1) * CHUNK)
        is_sample = i * GATE_CHUNKS + c >= n_prompt_chunks
        sel = jnp.where(is_sample, 1, 0)
        vn = _layernorm_rows(z_ref[rs, D_C:2 * D_C], lg_ref[...], lb_ref[...])
        vb = vn.astype(BF16)
        keep = (col <= row) & (col >= jnp.where(is_sample, row, 0))
        for h in range(H_C):
            hs = slice(h * DH_C, (h + 1) * DH_C)
            w = jnp.where(keep, ws_ref[sel, h], 0.0).astype(BF16)
            s = jnp.dot(w, vb[:, hs], preferred_element_type=F32) + bs_ref[sel][:, h:h + 1]
            o_ref[rs, hs] = (z_ref[rs, hs] * s).astype(BF16)

        if c == GATE_CHUNKS - 1:
            @pl.when(i == pl.num_programs(0) - 1)
            def _():
                v_ref[...] = vn


def _gate(z, lg, lb, ws_all, bs_all, n_prompt_chunks):
    rows = z.shape[0]
    tile = GATE_CHUNKS * CHUNK
    assert rows % tile == 0 and rows // CHUNK == n_prompt_chunks + 1
    return pl.pallas_call(
        functools.partial(_gate_kernel, n_prompt_chunks=n_prompt_chunks),
        out_shape=(jax.ShapeDtypeStruct((rows, D_C), BF16),
                   jax.ShapeDtypeStruct((CHUNK, D_C), F32)),
        grid=(rows // tile,),
        in_specs=[
            pl.BlockSpec((tile, 2 * D_C), lambda i: (i, 0)),
            pl.BlockSpec((1, D_C), lambda i: (0, 0)),
            pl.BlockSpec((1, D_C), lambda i: (0, 0)),
            pl.BlockSpec((2, H_C, CHUNK, CHUNK), lambda i: (0, 0, 0, 0)),
            pl.BlockSpec((2, CHUNK, H_C), lambda i: (0, 0, 0)),
        ],
        out_specs=(pl.BlockSpec((tile, D_C), lambda i: (i, 0)),
                   pl.BlockSpec((CHUNK, D_C), lambda i: (0, 0))),
        compiler_params=_params(("arbitrary",)),
        name="gate",
    )(z, lg, lb, ws_all, bs_all)


def kernel(x_prompt, x_sample, state_conv_a, state_conv_b, norm_g, ffn_w1, ffn_w3, ffn_w2,
           ab_w_in, a_conv_w, a_conv_b, a_ln_g, a_ln_b, b_conv_w, ab_w_out,
           c_w_in, c_b_in, c_ln_g, c_ln_b, c_w_s, c_b_s, c_w_out, final_g):
    batch, seq, _ = x_prompt.shape
    nb, dec_seq, _ = x_sample.shape
    depth = norm_g.shape[0]
    n_prompt = batch * seq
    rows = n_prompt + nb
    assert dec_seq == 1 and nb == CHUNK and n_prompt % CHUNK == 0

    row = lambda v: v.reshape(1, -1)
    ffn_stacked = (ffn_w1, ffn_w3, ffn_w2)
    w_bf16 = [w[0, 0].astype(BF16) for w in ffn_stacked]

    xs = [x_prompt.reshape(n_prompt, D_MODEL), x_sample.reshape(nb, D_MODEL)]
    new_a_p, new_b_p, new_a_s, new_b_s, new_v_s = [], [], [], [], []
    for i in range(depth):
        j = i // 2
        (x,), xn, w_bf16 = _ffn(xs, row(norm_g[i, 0]), *w_bf16, row(norm_g[i, 1]), (rows,),
                                next_w=(ffn_stacked, (i, 1)), emit_norm=True)
        if i % 2 == 0:
            ac, hb, sa_p, sb_p, sa_s, sb_s = _mixer_in(
                xn, ab_w_in[j], a_conv_w[j], row(a_conv_b[j]), b_conv_w[j],
                state_conv_a, state_conv_b, j, batch, seq)
            new_a_p.append(sa_p)
            new_b_p.append(sb_p)
            new_a_s.append(sa_s)
            new_b_s.append(sb_s)
            x = _mixer_out(ac, hb, row(a_ln_g[j]), row(a_ln_b[j]), ab_w_out[j], x)
        else:
            z = _proj(xn, c_w_in[j], row(c_b_in[j]), True)
            w_s = c_w_s[j][:, :CHUNK, :CHUNK]
            ws_all = jnp.stack([w_s, jnp.broadcast_to(w_s[:, :1, :1], w_s.shape)])
            b_s = c_b_s[j][:, :CHUNK].T
            bs_all = jnp.stack([b_s, jnp.broadcast_to(b_s[:1], b_s.shape)])
            h, v_s = _gate(z, row(c_ln_g[j]), row(c_ln_b[j]), ws_all, bs_all, n_prompt // CHUNK)
            new_v_s.append(v_s.reshape(nb, 1, D_C))
            x = _proj_residual(h, c_w_out[j], x)
        last = i == depth - 1
        xs, _, w_bf16 = _ffn([x], row(norm_g[i, 2]), *w_bf16, row(final_g),
                             (n_prompt, nb) if last else (rows,),
                             next_w=None if last else (ffn_stacked, (i + 1, 0)), final_norm=last)

    y_prompt, y_sample = xs
    return (y_prompt.reshape(batch, seq, D_MODEL), y_sample.reshape(nb, 1, D_MODEL),
            jnp.concatenate(new_a_p), jnp.concatenate(new_b_p),
            jnp.concatenate(new_a_s), jnp.concatenate(new_b_s),
            jnp.stack(new_v_s))
```

```python
import functools

import jax
import jax.numpy as jnp
from jax import lax
from jax.experimental import pallas as pl
from jax.experimental.pallas import tpu as pltpu

F32 = jnp.float32
BF16 = jnp.bfloat16

D_MODEL = 2048
D_FF = 5632
D_A = 1024
D_B = 1024
K_A = 31
K_B = 3
D_C = 2048
H_C = 8
DH_C = D_C // H_C
CHUNK = 128
EPS = 1e-6

VMEM_LIMIT_BYTES = 56 * 1024 * 1024
LANES = 128
SUBLANES = 8

TM = 1040
NORM_ROWS = 208
TF = 512
TN = 1024
W_STAGE = 256
MIX_T = 1024
MIX_C = 256
SAMPLE_BS = 64
CONV_RB = 64
A_HALO = 32
B_HALO = 8
GATE_CHUNKS = 5


def _params(sem):
    return pltpu.CompilerParams(dimension_semantics=sem, vmem_limit_bytes=VMEM_LIMIT_BYTES)


def _rms_rows(x, g):
    ms = jnp.mean(x * x, axis=-1, keepdims=True)
    return x * lax.rsqrt(ms + EPS) * g


def _layernorm_rows(x, g, b):
    mu = jnp.mean(x, axis=-1, keepdims=True)
    xc = x - mu
    var = jnp.mean(xc * xc, axis=-1, keepdims=True)
    return xc * lax.rsqrt(var + EPS) * g + b


def _norm_chunks(src_ref, g_ref, dst_ref, dtype):
    def body(c, carry):
        r = pl.ds(pl.multiple_of(c * NORM_ROWS, NORM_ROWS), NORM_ROWS)
        dst_ref[r, :] = _rms_rows(src_ref[r, :], g_ref[...]).astype(dtype)
        return carry
    lax.fori_loop(0, TM // NORM_ROWS, body, 0)


def _tile_pieces(tile, seg_rows):
    lo, hi = tile * TM, (tile + 1) * TM
    pieces, base = [], 0
    for k, n in enumerate(seg_rows):
        a, b = max(lo, base), min(hi, base + n)
        if a < b:
            pieces.append((k, a - base, a - lo, b - a))
        base += n
    return pieces


def _ffn_kernel(*refs, in_rows, out_rows, convert_next, emit_norm, final_norm):
    n_in, n_out = len(in_rows), len(out_rows)
    it = iter(refs)
    x_srcs = [next(it) for _ in range(n_in)]
    g_ref, w1_ref, w3_ref, w2_ref, post_g_ref = (next(it) for _ in range(5))
    next_w = [next(it) for _ in range(3)] if convert_next else []
    o_dsts = [next(it) for _ in range(n_out)]
    xn_out = next(it) if emit_norm else None
    next_w_out = [next(it) for _ in range(3)] if convert_next else []
    acc_ref, xn_ref, load_sem, store_sem = (next(it) for _ in range(4))
    xn_next_ref, xn_sem = (next(it), next(it)) if emit_norm else (None, None)

    i = pl.program_id(0)
    j = pl.program_id(1)
    n_i = pl.num_programs(0)
    n_j = pl.num_programs(1)
    n_tiles = sum(in_rows) // TM

    def load_copies(t):
        return [pltpu.make_async_copy(x_srcs[k].at[pl.ds(r0, n)],
                                      acc_ref.at[t % 2, pl.ds(d0, n)], load_sem.at[t % 2])
                for k, r0, d0, n in _tile_pieces(t, in_rows)]

    def store_copies(t):
        return [pltpu.make_async_copy(acc_ref.at[t % 2, pl.ds(d0, n)],
                                      o_dsts[k].at[pl.ds(r0, n)], store_sem.at[t % 2])
                for k, r0, d0, n in _tile_pieces(t, out_rows)]

    def xn_copy(t):
        return pltpu.make_async_copy(xn_next_ref, xn_out.at[pl.ds(t * TM, TM)], xn_sem)

    def on_tile(tile, fn):
        for t in range(n_tiles):
            @pl.when(tile == t)
            def _(t=t):
                fn(t)

    acc = acc_ref.at[lax.rem(i, 2)]

    @pl.when(j == 0)
    def _():
        @pl.when(i == 0)
        def _():
            for cp in load_copies(0):
                cp.start()
        on_tile(i, lambda t: [cp.wait() for cp in load_copies(t)])
        _norm_chunks(acc, g_ref, xn_ref, BF16)

    @pl.when(j == 2)
    def _():
        @pl.when(i > 0)
        def _():
            on_tile(i - 1, lambda t: [cp.wait() for cp in store_copies(t)])
            if emit_norm:
                on_tile(i - 1, lambda t: xn_copy(t).wait())

        @pl.when(i + 1 < n_i)
        def _():
            on_tile(i + 1, lambda t: [cp.start() for cp in load_copies(t)])

    xn = xn_ref[...]
    h1 = jnp.dot(xn, w1_ref[...], preferred_element_type=F32)
    h3 = jnp.dot(xn, w3_ref[...], preferred_element_type=F32)
    h = (0.5 * (h1 * jax.nn.sigmoid(h1)) * h3).astype(BF16)
    for n in range(D_MODEL // TF):
        cs = slice(n * TF, (n + 1) * TF)
        acc[:, cs] += jnp.dot(h, w2_ref[:, cs], preferred_element_type=F32)

    if convert_next:
        for src, dst in zip(next_w, next_w_out):
            dst[...] = src[...].astype(BF16)

    @pl.when(j == n_j - 1)
    def _():
        if final_norm:
            _norm_chunks(acc, post_g_ref, acc, F32)
        on_tile(i, lambda t: [cp.start() for cp in store_copies(t)])
        if emit_norm:
            _norm_chunks(acc, post_g_ref, xn_next_ref, BF16)
            on_tile(i, lambda t: xn_copy(t).start())

        @pl.when(i == n_i - 1)
        def _():
            on_tile(i, lambda t: [cp.wait() for cp in store_copies(t)])
            if emit_norm:
                on_tile(i, lambda t: xn_copy(t).wait())


def _ffn(xs, g, w1, w3, w2, post_g, out_rows, next_w=None, emit_norm=False, final_norm=False):
    in_rows = tuple(a.shape[0] for a in xs)
    rows = sum(in_rows)
    assert rows == sum(out_rows) and rows % TM == 0 and D_FF % TF == 0
    n_i, n_j = rows // TM, D_FF // TF
    assert n_j > 2 and D_MODEL % n_i == 0 and TF % n_i == 0
    convert_next = next_w is not None
    any_spec = pl.BlockSpec(memory_space=pl.ANY)
    row_spec = pl.BlockSpec((1, D_MODEL), lambda i, j: (0, 0))
    up_piece = pl.BlockSpec((D_MODEL // n_i, TF), lambda i, j: (i, j))
    down_piece = pl.BlockSpec((TF, D_MODEL // n_i), lambda i, j: (j, i))

    in_specs = [any_spec] * len(xs) + [
        row_spec,
        pl.BlockSpec((D_MODEL, TF), lambda i, j: (0, j)),
        pl.BlockSpec((D_MODEL, TF), lambda i, j: (0, j)),
        pl.BlockSpec((TF, D_MODEL), lambda i, j: (j, 0)),
        row_spec,
    ]
    args = list(xs) + [g, w1, w3, w2, post_g]
    out_shape = [jax.ShapeDtypeStruct((n, D_MODEL), F32) for n in out_rows]
    out_specs = [any_spec] * len(out_rows)
    if emit_norm:
        out_shape.append(jax.ShapeDtypeStruct((rows, D_MODEL), BF16))
        out_specs.append(any_spec)
    if convert_next:
        stacked, (li, si) = next_w
        in_specs += [
            pl.BlockSpec((None, None, D_MODEL // n_i, TF), lambda i, j: (li, si, i, j)),
            pl.BlockSpec((None, None, D_MODEL // n_i, TF), lambda i, j: (li, si, i, j)),
            pl.BlockSpec((None, None, TF, D_MODEL // n_i), lambda i, j: (li, si, j, i)),
        ]
        args += list(stacked)
        out_shape += [jax.ShapeDtypeStruct(w.shape[2:], BF16) for w in stacked]
        out_specs += [up_piece, up_piece, down_piece]
    scratch = [pltpu.VMEM((2, TM, D_MODEL), F32), pltpu.VMEM((TM, D_MODEL), BF16),
               pltpu.SemaphoreType.DMA((2,)), pltpu.SemaphoreType.DMA((2,))]
    if emit_norm:
        scratch += [pltpu.VMEM((TM, D_MODEL), BF16), pltpu.SemaphoreType.DMA(())]

    res = pl.pallas_call(
        functools.partial(_ffn_kernel, in_rows=in_rows, out_rows=tuple(out_rows),
                          convert_next=convert_next, emit_norm=emit_norm, final_norm=final_norm),
        out_shape=out_shape,
        grid=(n_i, n_j),
        in_specs=in_specs,
        out_specs=out_specs,
        scratch_shapes=scratch,
        compiler_params=_params(("arbitrary", "arbitrary")),
        name="ffn",
    )(*args)
    res = list(res)
    outs = [res.pop(0) for _ in out_rows]
    xn = res.pop(0) if emit_norm else None
    return outs, xn, (res if convert_next else None)


def _proj_kernel(a_ref, w_ref, b_ref, o_ref, wb_ref, *, gelu):
    @pl.when(pl.program_id(1) == 0)
    def _():
        wb_ref[...] = w_ref[...].astype(BF16)

    y = jnp.dot(a_ref[...], wb_ref[...], preferred_element_type=F32)
    if gelu:
        y = y + b_ref[...]
        y = 0.5 * y * (1.0 + lax.erf(y * (2.0 ** -0.5)))
    o_ref[...] = y


def _proj(a, w, b, gelu):
    rows, k = a.shape
    n = w.shape[1]
    return pl.pallas_call(
        functools.partial(_proj_kernel, gelu=gelu),
        out_shape=jax.ShapeDtypeStruct((rows, n), F32),
        grid=(n // TN, rows // TM),
        in_specs=[
            pl.BlockSpec((TM, k), lambda j, i: (i, 0)),
            pl.BlockSpec((k, TN), lambda j, i: (0, j)),
            pl.BlockSpec((1, TN), lambda j, i: (0, j)),
        ],
        out_specs=pl.BlockSpec((TM, TN), lambda j, i: (i, j)),
        scratch_shapes=[pltpu.VMEM((k, TN), BF16)],
        compiler_params=_params(("arbitrary", "arbitrary")),
        name="proj_gelu" if gelu else "proj",
    )(a, w, b)


def _proj_residual_kernel(h_ref, w_ref, x_ref, o_ref, wb_ref):
    @pl.when(pl.program_id(1) == 0)
    def _():
        wb_ref[...] = w_ref[...].astype(BF16)

    o_ref[...] = x_ref[...] + jnp.dot(h_ref[...], wb_ref[...], preferred_element_type=F32)


def _proj_residual(h, w, x):
    rows, k = h.shape
    return pl.pallas_call(
        _proj_residual_kernel,
        out_shape=jax.ShapeDtypeStruct((rows, D_MODEL), F32),
        grid=(D_MODEL // TN, rows // TM),
        in_specs=[
            pl.BlockSpec((TM, k), lambda j, i: (i, 0)),
            pl.BlockSpec((k, TN), lambda j, i: (0, j)),
            pl.BlockSpec((TM, TN), lambda j, i: (i, j)),
        ],
        out_specs=pl.BlockSpec((TM, TN), lambda j, i: (i, j)),
        scratch_shapes=[pltpu.VMEM((k, TN), BF16)],
        compiler_params=_params(("arbitrary", "arbitrary")),
        name="proj_residual",
    )(h, w, x)


def _conv_a(abuf, sbuf, wa_ref, n_rows):
    shift_rows = A_HALO + n_rows - SUBLANES
    for s in range(1, SUBLANES):
        sbuf[s - 1, 0:shift_rows, :] = abuf[s:s + shift_rows, :]

    a0 = A_HALO - (K_A - 1)

    def tap(k, r0, cs):
        q, s = divmod(a0 + k, SUBLANES)
        src = abuf if s == 0 else sbuf.at[s - 1]
        return wa_ref[k:k + 1, cs] * src[SUBLANES * q + r0:SUBLANES * q + r0 + CONV_RB, cs]

    for c in range(MIX_C // LANES):
        cs = slice(c * LANES, (c + 1) * LANES)
        for r0 in range(0, n_rows, CONV_RB):
            acc = tap(0, r0, cs)
            for k in range(1, K_A):
                acc = acc + tap(k, r0, cs)
            yield slice(r0, r0 + CONV_RB), cs, acc


def _load_bf16_columns(w_hbm, col_starts, width, dsts, stage, sem):
    def chunk(c):
        return pltpu.make_async_copy(w_hbm.at[:, pl.ds(col_starts[c], width)], stage.at[c % 2],
                                     sem.at[c % 2])
    chunk(0).start()
    for c in range(len(col_starts)):
        if c + 1 < len(col_starts):
            chunk(c + 1).start()
        chunk(c).wait()
        dsts[c][...] = stage[c % 2].astype(BF16)


def _mixer_in_kernel(xn_ref, w_hbm, wa_ref, ba_ref, wb_ref, sa_ref, sb_ref,
                     ac_ref, hb_ref, nap_ref, nbp_ref, nas_ref, nbs_ref,
                     wbf, abuf, sbuf, bbuf, stage, sem, *, n_prompt_tiles, tiles_per_seq):
    c_tile = pl.program_id(0)
    r = pl.program_id(1)
    n_groups = 5
    wcur = wbf.at[lax.rem(c_tile, 2)]
    wnext = wbf.at[1 - lax.rem(c_tile, 2)]

    @pl.when((c_tile == 0) & (r == 0))
    def _():
        _load_bf16_columns(w_hbm, [g * D_A for g in range(n_groups)], MIX_C,
                           [wcur.at[:, pl.ds(g * MIX_C, MIX_C)] for g in range(n_groups)],
                           stage, sem)

    @pl.when(c_tile + 1 < pl.num_programs(0))
    def _():
        col = pl.multiple_of((c_tile + 1) * MIX_C, MIX_C)

        def fetch(g):
            return pltpu.make_async_copy(w_hbm.at[:, pl.ds(g * D_A + col, MIX_C)],
                                         stage.at[g % 2], sem.at[g % 2])
        for g in range(n_groups):
            @pl.when(r == g + 2)
            def _(g=g):
                fetch(g).wait()
                wnext[:, g * MIX_C:(g + 1) * MIX_C] = stage[g % 2].astype(BF16)

            @pl.when(r == g + 1)
            def _(g=g):
                fetch(g).start()

    def project(xn):
        p = jnp.dot(xn, wcur[...], preferred_element_type=F32)
        return [p[:, g * MIX_C:(g + 1) * MIX_C] for g in range(n_groups)]

    @pl.when(r < n_prompt_tiles)
    def _():
        @pl.when(lax.rem(r, tiles_per_seq) == 0)
        def _():
            abuf[0:A_HALO, :] = jnp.zeros((A_HALO, MIX_C), F32)
            bbuf[0:B_HALO, :] = jnp.zeros((B_HALO, MIX_C), F32)

        pa, ga, bg, cg, hb = project(xn_ref[...])
        abuf[A_HALO:A_HALO + MIX_T, :] = pa * jax.nn.sigmoid(ga)
        bbuf[B_HALO:B_HALO + MIX_T, :] = cg * hb

        for rs, cs, acc in _conv_a(abuf, sbuf, wa_ref, MIX_T):
            ac_ref[rs, cs] = acc + ba_ref[:, cs]

        b0 = B_HALO - (K_B - 1)
        b_c = wb_ref[0:1, :] * bbuf[b0:b0 + MIX_T, :]
        for k in range(1, K_B):
            b_c = b_c + wb_ref[k:k + 1, :] * bbuf[b0 + k:b0 + k + MIX_T, :]
        hb_ref[...] = (bg * b_c).astype(BF16)

        @pl.when(lax.rem(r, tiles_per_seq) == tiles_per_seq - 1)
        def _():
            nap_ref[0] = abuf[MIX_T + A_HALO - (K_A - 1):MIX_T + A_HALO, :]
            nbp_ref[0] = bbuf[MIX_T + b0:MIX_T + B_HALO, :]

        abuf[0:A_HALO, :] = abuf[MIX_T:MIX_T + A_HALO, :]
        bbuf[0:B_HALO, :] = bbuf[MIX_T:MIX_T + B_HALO, :]

    @pl.when(r >= n_prompt_tiles)
    def _():
        rs = pl.ds(pl.multiple_of((r - n_prompt_tiles) * SAMPLE_BS, SAMPLE_BS), SAMPLE_BS)
        pa, ga, bg, cg, hb = project(xn_ref[rs, :])
        a = pa * jax.nn.sigmoid(ga)
        acc = wa_ref[K_A - 1:K_A, :] * a
        for k in range(K_A - 1):
            acc = acc + wa_ref[k:k + 1, :] * sa_ref[k]
        ac_ref[rs, :] = acc + ba_ref[...]
        for k in range(K_A - 2):
            nas_ref[k] = sa_ref[k + 1]
        nas_ref[K_A - 2] = a

        ch = cg * hb
        b_c = wb_ref[K_B - 1:K_B, :] * ch
        for k in range(K_B - 1):
            b_c = b_c + wb_ref[k:k + 1, :] * sb_ref[:, k, :]
        hb_ref[rs, :] = (bg * b_c).astype(BF16)
        for k in range(K_B - 2):
            nbs_ref[:, k, :] = sb_ref[:, k + 1, :]
        nbs_ref[:, K_B - 2, :] = ch


def _mixer_in(xn, w_in, wa, ba, wb, sa, sb, layer, batch, seq):
    rows = xn.shape[0]
    n_sample = sa.shape[2]
    n_prompt_tiles = batch * seq // MIX_T
    tiles_per_seq = seq // MIX_T
    assert rows == batch * seq + n_sample and n_sample <= MIX_T and seq % MIX_T == 0
    assert n_sample % SAMPLE_BS == 0
    assert n_prompt_tiles + n_sample // SAMPLE_BS >= 7
    row_blk = lambda r: jnp.minimum(r, n_prompt_tiles)
    smp_blk = lambda r: jnp.maximum(r - n_prompt_tiles, 0)
    seq_of = lambda r: jnp.minimum(r, n_prompt_tiles - 1) // tiles_per_seq
    chan = lambda k: pl.BlockSpec((k, MIX_C), lambda c, r: (0, c))
    state = lambda k, lyr: pl.BlockSpec((None, SAMPLE_BS, k, MIX_C),
                                        lambda c, r: (lyr, smp_blk(r), 0, c))
    tap_major = lambda k, lyr: pl.BlockSpec((None, k, SAMPLE_BS, MIX_C),
                                            lambda c, r: (lyr, 0, smp_blk(r), c))
    prompt_state = lambda k: pl.BlockSpec((None, 1, k, MIX_C), lambda c, r: (0, seq_of(r), 0, c))
    return pl.pallas_call(
        functools.partial(_mixer_in_kernel, n_prompt_tiles=n_prompt_tiles,
                          tiles_per_seq=tiles_per_seq),
        out_shape=(jax.ShapeDtypeStruct((rows, D_A), F32),
                   jax.ShapeDtypeStruct((rows, D_B), BF16),
                   jax.ShapeDtypeStruct((1, batch, K_A - 1, D_A), F32),
                   jax.ShapeDtypeStruct((1, batch, K_B - 1, D_B), F32),
                   jax.ShapeDtypeStruct((1, K_A - 1, n_sample, D_A), F32),
                   jax.ShapeDtypeStruct((1, n_sample, K_B - 1, D_B), F32)),
        grid=(D_A // MIX_C, n_prompt_tiles + n_sample // SAMPLE_BS),
        in_specs=[pl.BlockSpec((MIX_T, D_MODEL), lambda c, r: (row_blk(r), 0)),
                  pl.BlockSpec(memory_space=pl.ANY),
                  chan(K_A), chan(1), chan(K_B),
                  tap_major(K_A - 1, layer), state(K_B - 1, layer)],
        out_specs=(pl.BlockSpec((MIX_T, MIX_C), lambda c, r: (row_blk(r), c)),
                   pl.BlockSpec((MIX_T, MIX_C), lambda c, r: (row_blk(r), c)),
                   prompt_state(K_A - 1), prompt_state(K_B - 1),
                   tap_major(K_A - 1, 0), state(K_B - 1, 0)),
        scratch_shapes=[pltpu.VMEM((2, D_MODEL, 5 * MIX_C), BF16),
                        pltpu.VMEM((A_HALO + MIX_T, MIX_C), F32),
                        pltpu.VMEM((SUBLANES - 1, A_HALO + MIX_T - SUBLANES, MIX_C), F32),
                        pltpu.VMEM((B_HALO + MIX_T, MIX_C), F32),
                        pltpu.VMEM((2, D_MODEL, MIX_C), F32),
                        pltpu.SemaphoreType.DMA((2,))],
        compiler_params=_params(("arbitrary", "arbitrary")),
        name="mixer_in",
    )(xn, w_in, wa, ba, wb, sa, sb)


def _mixer_out_kernel(ac_ref, hb_ref, lg_ref, lb_ref, w_hbm, x_ref, o_ref, wb_ref, stage, sem):
    j = pl.program_id(0)

    @pl.when(pl.program_id(1) == 0)
    def _():
        col = pl.multiple_of(j * TN, TN)
        offs = range(0, TN, W_STAGE)
        _load_bf16_columns(w_hbm, [col + o for o in offs], W_STAGE,
                           [wb_ref.at[:, pl.ds(o, W_STAGE)] for o in offs], stage, sem)

    a_ln = _layernorm_rows(ac_ref[...], lg_ref[...], lb_ref[...])
    h_a = (a_ln * jax.nn.sigmoid(a_ln)).astype(BF16)
    y = jnp.dot(h_a, wb_ref[0:D_A, :], preferred_element_type=F32)
    y = y + jnp.dot(hb_ref[...], wb_ref[D_A:D_A + D_B, :], preferred_element_type=F32)
    o_ref[...] = x_ref[...] + y


def _mixer_out(ac, hb, lg, lb, w, x):
    rows = x.shape[0]
    return pl.pallas_call(
        _mixer_out_kernel,
        out_shape=jax.ShapeDtypeStruct((rows, D_MODEL), F32),
        grid=(D_MODEL // TN, rows // TM),
        in_specs=[
            pl.BlockSpec((TM, D_A), lambda j, i: (i, 0)),
            pl.BlockSpec((TM, D_B), lambda j, i: (i, 0)),
            pl.BlockSpec((1, D_A), lambda j, i: (0, 0)),
            pl.BlockSpec((1, D_A), lambda j, i: (0, 0)),
            pl.BlockSpec(memory_space=pl.ANY),
            pl.BlockSpec((TM, TN), lambda j, i: (i, j)),
        ],
        out_specs=pl.BlockSpec((TM, TN), lambda j, i: (i, j)),
        scratch_shapes=[pltpu.VMEM((D_A + D_B, TN), BF16),
                        pltpu.VMEM((2, D_A + D_B, W_STAGE), F32),
                        pltpu.SemaphoreType.DMA((2,))],
        compiler_params=_params(("arbitrary", "arbitrary")),
        name="mixer_out",
    )(ac, hb, lg, lb, w, x)


def _gate_kernel(z_ref, lg_ref, lb_ref, ws_ref, bs_ref, o_ref, v_ref, *, n_prompt_chunks):
    i = pl.program_id(0)
    row = lax.broadcasted_iota(jnp.int32, (CHUNK, CHUNK), 0)
    col = lax.broadcasted_iota(jnp.int32, (CHUNK, CHUNK), 1)
    for c in range(GATE_CHUNKS):
        rs = slice(c * CHUNK, (c + 1) * CHUNK)
        is_sample = i * GATE_CHUNKS + c >= n_prompt_chunks
        sel = jnp.where(is_sample, 1, 0)
        vn = _layernorm_rows(z_ref[rs, D_C:2 * D_C], lg_ref[...], lb_ref[...])
        vb = vn.astype(BF16)
        keep = (col <= row) & (col >= jnp.where(is_sample, row, 0))
        for h in range(H_C):
            hs = slice(h * DH_C, (h + 1) * DH_C)
            w = jnp.where(keep, ws_ref[sel, h], 0.0).astype(BF16)
            s = jnp.dot(w, vb[:, hs], preferred_element_type=F32) + bs_ref[sel][:, h:h + 1]
            o_ref[rs, hs] = (z_ref[rs, hs] * s).astype(BF16)

        if c == GATE_CHUNKS - 1:
            @pl.when(i == pl.num_programs(0) - 1)
            def _():
                v_ref[...] = vn


def _gate(z, lg, lb, ws_all, bs_all, n_prompt_chunks):
    rows = z.shape[0]
    tile = GATE_CHUNKS * CHUNK
    assert rows % tile == 0 and rows // CHUNK == n_prompt_chunks + 1
    return pl.pallas_call(
        functools.partial(_gate_kernel, n_prompt_chunks=n_prompt_chunks),
        out_shape=(jax.ShapeDtypeStruct((rows, D_C), BF16),
                   jax.ShapeDtypeStruct((CHUNK, D_C), F32)),
        grid=(rows // tile,),
        in_specs=[
            pl.BlockSpec((tile, 2 * D_C), lambda i: (i, 0)),
            pl.BlockSpec((1, D_C), lambda i: (0, 0)),
            pl.BlockSpec((1, D_C), lambda i: (0, 0)),
            pl.BlockSpec((2, H_C, CHUNK, CHUNK), lambda i: (0, 0, 0, 0)),
            pl.BlockSpec((2, CHUNK, H_C), lambda i: (0, 0, 0)),
        ],
        out_specs=(pl.BlockSpec((tile, D_C), lambda i: (i, 0)),
                   pl.BlockSpec((CHUNK, D_C), lambda i: (0, 0))),
        compiler_params=_params(("arbitrary",)),
        name="gate",
    )(z, lg, lb, ws_all, bs_all)


def kernel(x_prompt, x_sample, state_conv_a, state_conv_b, norm_g, ffn_w1, ffn_w3, ffn_w2,
           ab_w_in, a_conv_w, a_conv_b, a_ln_g, a_ln_b, b_conv_w, ab_w_out,
           c_w_in, c_b_in, c_ln_g, c_ln_b, c_w_s, c_b_s, c_w_out, final_g):
    batch, seq, _ = x_prompt.shape
    nb, dec_seq, _ = x_sample.shape
    depth = norm_g.shape[0]
    n_prompt = batch * seq
    rows = n_prompt + nb
    assert dec_seq == 1 and nb == CHUNK and n_prompt % CHUNK == 0

    row = lambda v: v.reshape(1, -1)
    ffn_stacked = (ffn_w1, ffn_w3, ffn_w2)
    w_bf16 = [w[0, 0].astype(BF16) for w in ffn_stacked]

    xs = [x_prompt.reshape(n_prompt, D_MODEL), x_sample.reshape(nb, D_MODEL)]
    state_a_taps = jnp.swapaxes(state_conv_a, 1, 2)
    new_a_p, new_b_p, new_a_s, new_b_s, new_v_s = [], [], [], [], []
    for i in range(depth):
        j = i // 2
        (x,), xn, w_bf16 = _ffn(xs, row(norm_g[i, 0]), *w_bf16, row(norm_g[i, 1]), (rows,),
                                next_w=(ffn_stacked, (i, 1)), emit_norm=True)
        if i % 2 == 0:
            ac, hb, sa_p, sb_p, sa_s, sb_s = _mixer_in(
                xn, ab_w_in[j], a_conv_w[j], row(a_conv_b[j]), b_conv_w[j],
                state_a_taps, state_conv_b, j, batch, seq)
            new_a_p.append(sa_p)
            new_b_p.append(sb_p)
            new_a_s.append(jnp.swapaxes(sa_s, 1, 2))
            new_b_s.append(sb_s)
            x = _mixer_out(ac, hb, row(a_ln_g[j]), row(a_ln_b[j]), ab_w_out[j], x)
        else:
            z = _proj(xn, c_w_in[j], row(c_b_in[j]), True)
            w_s = c_w_s[j][:, :CHUNK, :CHUNK]
            ws_all = jnp.stack([w_s, jnp.broadcast_to(w_s[:, :1, :1], w_s.shape)])
            b_s = c_b_s[j][:, :CHUNK].T
            bs_all = jnp.stack([b_s, jnp.broadcast_to(b_s[:1], b_s.shape)])
            h, v_s = _gate(z, row(c_ln_g[j]), row(c_ln_b[j]), ws_all, bs_all, n_prompt // CHUNK)
            new_v_s.append(v_s.reshape(nb, 1, D_C))
            x = _proj_residual(h, c_w_out[j], x)
        last = i == depth - 1
        xs, _, w_bf16 = _ffn([x], row(norm_g[i, 2]), *w_bf16, row(final_g),
                             (n_prompt, nb) if last else (rows,),
                             next_w=None if last else (ffn_stacked, (i + 1, 0)), final_norm=last)

    y_prompt, y_sample = xs
    return (y_prompt.reshape(batch, seq, D_MODEL), y_sample.reshape(nb, 1, D_MODEL),
            jnp.concatenate(new_a_p), jnp.concatenate(new_b_p),
            jnp.concatenate(new_a_s), jnp.concatenate(new_b_s),
            jnp.stack(new_v_s))
```

```python
import functools

import jax
import jax.numpy as jnp
from jax import lax
from jax.experimental import pallas as pl
from jax.experimental.pallas import tpu as pltpu

F32 = jnp.float32
BF16 = jnp.bfloat16

D_MODEL = 2048
D_FF = 5632
D_A = 1024
D_B = 1024
K_A = 31
K_B = 3
D_C = 2048
H_C = 8
DH_C = D_C // H_C
CHUNK = 128
EPS = 1e-6

VMEM_LIMIT_BYTES = 56 * 1024 * 1024
LANES = 128
SUBLANES = 8

TM = 1040
NORM_ROWS = 208
TF = 512
TF_F32 = 256
DOWN_CHUNK = 512
TN = 1024
W_STAGE = 256
MIX_T = 1024
MIX_C = 256
SAMPLE_BS = 64
CONV_RB = 64
A_HALO = 32
B_HALO = 8
GATE_CHUNKS = 5


def _params(sem):
    return pltpu.CompilerParams(dimension_semantics=sem, vmem_limit_bytes=VMEM_LIMIT_BYTES)


def _rms_rows(x, g):
    ms = jnp.mean(x * x, axis=-1, keepdims=True)
    return x * lax.rsqrt(ms + EPS) * g


def _layernorm_rows(x, g, b):
    mu = jnp.mean(x, axis=-1, keepdims=True)
    xc = x - mu
    var = jnp.mean(xc * xc, axis=-1, keepdims=True)
    return xc * lax.rsqrt(var + EPS) * g + b


def _norm_chunks(src_ref, g_ref, dst_ref, dtype):
    def body(c, carry):
        r = pl.ds(pl.multiple_of(c * NORM_ROWS, NORM_ROWS), NORM_ROWS)
        dst_ref[r, :] = _rms_rows(src_ref[r, :], g_ref[...]).astype(dtype)
        return carry
    lax.fori_loop(0, TM // NORM_ROWS, body, 0)


def _tile_pieces(tile, seg_rows):
    lo, hi = tile * TM, (tile + 1) * TM
    pieces, base = [], 0
    for k, n in enumerate(seg_rows):
        a, b = max(lo, base), min(hi, base + n)
        if a < b:
            pieces.append((k, a - base, a - lo, b - a))
        base += n
    return pieces


def _ffn_kernel(*refs, in_rows, out_rows, tile0, n_tiles, own_f32, convert_next, emit_norm,
                final_norm, head_rows):
    n_in, n_out = len(in_rows), len(out_rows)
    it = iter(refs)
    x_srcs = [next(it) for _ in range(n_in)]
    g_ref, w1_ref, w3_ref, w2_ref, post_g_ref = (next(it) for _ in range(5))
    next_w = [next(it) for _ in range(3)] if convert_next else []
    head_srcs = [next(it) for _ in range(2)] if head_rows else []
    o_dsts = [next(it) for _ in range(n_out)]
    xn_out = next(it) if emit_norm else None
    own_w_out = [next(it) for _ in range(3)] if own_f32 else []
    next_w_out = [next(it) for _ in range(3)] if convert_next else []
    acc_ref, xn_ref, load_sem, store_sem = (next(it) for _ in range(4))
    xn_next_ref, xn_sem = (next(it), next(it)) if emit_norm else (None, None)
    head_sem = next(it) if head_rows else None

    i = pl.program_id(0)
    j = pl.program_id(1)
    n_i = pl.num_programs(0)
    n_j = pl.num_programs(1)
    tile = i + tile0

    def load_copies(t):
        return [pltpu.make_async_copy(x_srcs[k].at[pl.ds(r0, n)],
                                      acc_ref.at[t % 2, pl.ds(d0, n)], load_sem.at[t % 2])
                for k, r0, d0, n in _tile_pieces(t, in_rows)]

    def store_copies(t):
        return [pltpu.make_async_copy(acc_ref.at[t % 2, pl.ds(d0, n)],
                                      o_dsts[k].at[pl.ds(r0, n)], store_sem.at[t % 2])
                for k, r0, d0, n in _tile_pieces(t, out_rows)]

    def xn_copy(t):
        return pltpu.make_async_copy(xn_next_ref, xn_out.at[pl.ds(t * TM, TM)], xn_sem)

    def head_copies():
        return [pltpu.make_async_copy(head_srcs[0], o_dsts[0].at[pl.ds(0, head_rows)], head_sem.at[0]),
                pltpu.make_async_copy(head_srcs[1], xn_out.at[pl.ds(0, head_rows)], head_sem.at[1])]

    def on_tile(which, fn):
        for t in range(tile0, tile0 + n_tiles):
            @pl.when(which == t)
            def _(t=t):
                fn(t)

    acc = acc_ref.at[lax.rem(tile, 2)]

    @pl.when(j == 0)
    def _():
        @pl.when(i == 0)
        def _():
            for cp in load_copies(tile0):
                cp.start()
            if head_rows:
                for cp in head_copies():
                    cp.start()
        on_tile(tile, lambda t: [cp.wait() for cp in load_copies(t)])
        _norm_chunks(acc, g_ref, xn_ref, BF16)

    @pl.when(j == 2)
    def _():
        @pl.when(i > 0)
        def _():
            on_tile(tile - 1, lambda t: [cp.wait() for cp in store_copies(t)])
            if emit_norm:
                on_tile(tile - 1, lambda t: xn_copy(t).wait())

        @pl.when(i + 1 < n_i)
        def _():
            on_tile(tile + 1, lambda t: [cp.start() for cp in load_copies(t)])

    if own_f32:
        w1, w3, w2 = (w[...].astype(BF16) for w in (w1_ref, w3_ref, w2_ref))
        for dst, w in zip(own_w_out, (w1, w3, w2)):
            dst[...] = w
        w2_cols = lambda cs: w2[:, cs]
    else:
        w1, w3 = w1_ref[...], w3_ref[...]
        w2_cols = lambda cs: w2_ref[:, cs]
    xn = xn_ref[...]
    h1 = jnp.dot(xn, w1, preferred_element_type=F32)
    h3 = jnp.dot(xn, w3, preferred_element_type=F32)
    h = (0.5 * (h1 * jax.nn.sigmoid(h1)) * h3).astype(BF16)
    for n in range(D_MODEL // DOWN_CHUNK):
        cs = slice(n * DOWN_CHUNK, (n + 1) * DOWN_CHUNK)
        acc[:, cs] += jnp.dot(h, w2_cols(cs), preferred_element_type=F32)

    if convert_next:
        for src, dst in zip(next_w, next_w_out):
            dst[...] = src[...].astype(BF16)

    @pl.when(j == n_j - 1)
    def _():
        if final_norm:
            _norm_chunks(acc, post_g_ref, acc, F32)
        on_tile(tile, lambda t: [cp.start() for cp in store_copies(t)])
        if emit_norm:
            _norm_chunks(acc, post_g_ref, xn_next_ref, BF16)
            on_tile(tile, lambda t: xn_copy(t).start())

        @pl.when(i == n_i - 1)
        def _():
            on_tile(tile, lambda t: [cp.wait() for cp in store_copies(t)])
            if emit_norm:
                on_tile(tile, lambda t: xn_copy(t).wait())
            if head_rows:
                for cp in head_copies():
                    cp.wait()


def _round_up(n, m):
    return -(-n // m) * m


def _ffn(xs, g, w_bf16, post_g, out_rows, *, tf=TF, tile0=0, n_tiles=None, own_f32=None,
         next_w=None, emit_norm=False, final_norm=False, head=None):
    in_rows = tuple(a.shape[0] for a in xs)
    rows = sum(in_rows)
    assert rows % TM == 0 and D_FF % tf == 0 and D_MODEL % DOWN_CHUNK == 0
    n_i = rows // TM - tile0 if n_tiles is None else n_tiles
    n_j = D_FF // tf
    out_total = sum(out_rows)
    assert n_j > 2 and out_total == (tile0 + n_i) * TM
    head_rows = 0 if head is None else head[0].shape[0]
    assert head_rows == (tile0 * TM if head is not None else 0) and (emit_norm or head is None)
    convert_next = next_w is not None
    any_spec = pl.BlockSpec(memory_space=pl.ANY)
    row_spec = pl.BlockSpec((1, D_MODEL), lambda i, j: (0, 0))

    if own_f32 is None:
        w_specs = [pl.BlockSpec((D_MODEL, tf), lambda i, j: (0, j)),
                   pl.BlockSpec((D_MODEL, tf), lambda i, j: (0, j)),
                   pl.BlockSpec((tf, D_MODEL), lambda i, j: (j, 0))]
        w_args = list(w_bf16)
    else:
        own_stacked, (oli, osi) = own_f32
        w_specs = [pl.BlockSpec((None, None, D_MODEL, tf), lambda i, j: (oli, osi, 0, j)),
                   pl.BlockSpec((None, None, D_MODEL, tf), lambda i, j: (oli, osi, 0, j)),
                   pl.BlockSpec((None, None, tf, D_MODEL), lambda i, j: (oli, osi, j, 0))]
        w_args = list(own_stacked)

    in_specs = [any_spec] * len(xs) + [row_spec] + w_specs + [row_spec]
    args = list(xs) + [g] + w_args + [post_g]
    out_shape = [jax.ShapeDtypeStruct((n, D_MODEL), F32) for n in out_rows]
    out_specs = [any_spec] * len(out_rows)
    if convert_next:
        stacked, (li, si) = next_w
        pr = _round_up(pl.cdiv(D_MODEL, n_i), 2 * SUBLANES)
        pd = _round_up(pl.cdiv(D_FF, n_i * n_j), 2 * SUBLANES)
        last_pd = pl.cdiv(D_FF, pd) - 1
        down_idx = lambda i, j: jnp.minimum(i * n_j + j, last_pd)
        assert (n_i - 1) * pr < D_MODEL
        in_specs += [
            pl.BlockSpec((None, None, pr, tf), lambda i, j: (li, si, i, j)),
            pl.BlockSpec((None, None, pr, tf), lambda i, j: (li, si, i, j)),
            pl.BlockSpec((None, None, pd, D_MODEL), lambda i, j: (li, si, down_idx(i, j), 0)),
        ]
        args += list(stacked)
    if head is not None:
        in_specs += [any_spec, any_spec]
        args += list(head)
    if emit_norm:
        out_shape.append(jax.ShapeDtypeStruct((out_total, D_MODEL), BF16))
        out_specs.append(any_spec)
    if own_f32 is not None:
        out_shape += [jax.ShapeDtypeStruct(w.shape[2:], BF16) for w in own_stacked]
        out_specs += [pl.BlockSpec((D_MODEL, tf), lambda i, j: (0, j)),
                      pl.BlockSpec((D_MODEL, tf), lambda i, j: (0, j)),
                      pl.BlockSpec((tf, D_MODEL), lambda i, j: (j, 0))]
    if convert_next:
        out_shape += [jax.ShapeDtypeStruct(w.shape[2:], BF16) for w in stacked]
        out_specs += [pl.BlockSpec((pr, tf), lambda i, j: (i, j)),
                      pl.BlockSpec((pr, tf), lambda i, j: (i, j)),
                      pl.BlockSpec((pd, D_MODEL), lambda i, j: (down_idx(i, j), 0))]
    scratch = [pltpu.VMEM((2, TM, D_MODEL), F32), pltpu.VMEM((TM, D_MODEL), BF16),
               pltpu.SemaphoreType.DMA((2,)), pltpu.SemaphoreType.DMA((2,))]
    if emit_norm:
        scratch += [pltpu.VMEM((TM, D_MODEL), BF16), pltpu.SemaphoreType.DMA(())]
    if head is not None:
        scratch += [pltpu.SemaphoreType.DMA((2,))]

    res = pl.pallas_call(
        functools.partial(_ffn_kernel, in_rows=in_rows, out_rows=tuple(out_rows), tile0=tile0,
                          n_tiles=n_i, own_f32=own_f32 is not None, convert_next=convert_next,
                          emit_norm=emit_norm, final_norm=final_norm, head_rows=head_rows),
        out_shape=out_shape,
        grid=(n_i, n_j),
        in_specs=in_specs,
        out_specs=out_specs,
        scratch_shapes=scratch,
        compiler_params=_params(("arbitrary", "arbitrary")),
        name="ffn",
    )(*args)
    res = list(res)
    outs = [res.pop(0) for _ in out_rows]
    xn = res.pop(0) if emit_norm else None
    own = [res.pop(0) for _ in range(3)] if own_f32 is not None else None
    return outs, xn, own, (res if convert_next else None)


def _proj_kernel(a_ref, w_ref, b_ref, o_ref, wb_ref, *, gelu):
    @pl.when(pl.program_id(1) == 0)
    def _():
        wb_ref[...] = w_ref[...].astype(BF16)

    y = jnp.dot(a_ref[...], wb_ref[...], preferred_element_type=F32)
    if gelu:
        y = y + b_ref[...]
        y = 0.5 * y * (1.0 + lax.erf(y * (2.0 ** -0.5)))
    o_ref[...] = y


def _proj(a, w, b, gelu):
    rows, k = a.shape
    n = w.shape[1]
    return pl.pallas_call(
        functools.partial(_proj_kernel, gelu=gelu),
        out_shape=jax.ShapeDtypeStruct((rows, n), F32),
        grid=(n // TN, rows // TM),
        in_specs=[
            pl.BlockSpec((TM, k), lambda j, i: (i, 0)),
            pl.BlockSpec((k, TN), lambda j, i: (0, j)),
            pl.BlockSpec((1, TN), lambda j, i: (0, j)),
        ],
        out_specs=pl.BlockSpec((TM, TN), lambda j, i: (i, j)),
        scratch_shapes=[pltpu.VMEM((k, TN), BF16)],
        compiler_params=_params(("arbitrary", "arbitrary")),
        name="proj_gelu" if gelu else "proj",
    )(a, w, b)


def _proj_residual_kernel(h_ref, w_ref, x_ref, o_ref, wb_ref):
    @pl.when(pl.program_id(1) == 0)
    def _():
        wb_ref[...] = w_ref[...].astype(BF16)

    o_ref[...] = x_ref[...] + jnp.dot(h_ref[...], wb_ref[...], preferred_element_type=F32)


def _proj_residual(h, w, x):
    rows, k = h.shape
    return pl.pallas_call(
        _proj_residual_kernel,
        out_shape=jax.ShapeDtypeStruct((rows, D_MODEL), F32),
        grid=(D_MODEL // TN, rows // TM),
        in_specs=[
            pl.BlockSpec((TM, k), lambda j, i: (i, 0)),
            pl.BlockSpec((k, TN), lambda j, i: (0, j)),
            pl.BlockSpec((TM, TN), lambda j, i: (i, j)),
        ],
        out_specs=pl.BlockSpec((TM, TN), lambda j, i: (i, j)),
        scratch_shapes=[pltpu.VMEM((k, TN), BF16)],
        compiler_params=_params(("arbitrary", "arbitrary")),
        name="proj_residual",
    )(h, w, x)


def _conv_a(abuf, sbuf, wa_ref, n_rows):
    shift_rows = A_HALO + n_rows - SUBLANES
    for s in range(1, SUBLANES):
        sbuf[s - 1, 0:shift_rows, :] = abuf[s:s + shift_rows, :]

    a0 = A_HALO - (K_A - 1)

    def tap(k, r0, cs):
        q, s = divmod(a0 + k, SUBLANES)
        src = abuf if s == 0 else sbuf.at[s - 1]
        return wa_ref[k:k + 1, cs] * src[SUBLANES * q + r0:SUBLANES * q + r0 + CONV_RB, cs]

    for c in range(MIX_C // LANES):
        cs = slice(c * LANES, (c + 1) * LANES)
        for r0 in range(0, n_rows, CONV_RB):
            acc = tap(0, r0, cs)
            for k in range(1, K_A):
                acc = acc + tap(k, r0, cs)
            yield slice(r0, r0 + CONV_RB), cs, acc


def _load_bf16_columns(w_hbm, col_starts, width, dsts, stage, sem):
    def chunk(c):
        return pltpu.make_async_copy(w_hbm.at[:, pl.ds(col_starts[c], width)], stage.at[c % 2],
                                     sem.at[c % 2])
    chunk(0).start()
    for c in range(len(col_starts)):
        if c + 1 < len(col_starts):
            chunk(c + 1).start()
        chunk(c).wait()
        dsts[c][...] = stage[c % 2].astype(BF16)


def _mixer_in_kernel(xn_ref, w_hbm, wa_ref, ba_ref, wb_ref, sa_ref, sb_ref,
                     ac_ref, hb_ref, nap_ref, nbp_ref, nas_ref, nbs_ref,
                     wbf, abuf, sbuf, bbuf, stage, sem, *, n_prompt_tiles, tiles_per_seq):
    c_tile = pl.program_id(0)
    r = pl.program_id(1)
    n_groups = 5
    wcur = wbf.at[lax.rem(c_tile, 2)]
    wnext = wbf.at[1 - lax.rem(c_tile, 2)]

    @pl.when((c_tile == 0) & (r == 0))
    def _():
        _load_bf16_columns(w_hbm, [g * D_A for g in range(n_groups)], MIX_C,
                           [wcur.at[:, pl.ds(g * MIX_C, MIX_C)] for g in range(n_groups)],
                           stage, sem)

    @pl.when(c_tile + 1 < pl.num_programs(0))
    def _():
        col = pl.multiple_of((c_tile + 1) * MIX_C, MIX_C)

        def fetch(g):
            return pltpu.make_async_copy(w_hbm.at[:, pl.ds(g * D_A + col, MIX_C)],
                                         stage.at[g % 2], sem.at[g % 2])
        for g in range(n_groups):
            @pl.when(r == g + 2)
            def _(g=g):
                fetch(g).wait()
                wnext[:, g * MIX_C:(g + 1) * MIX_C] = stage[g % 2].astype(BF16)

            @pl.when(r == g + 1)
            def _(g=g):
                fetch(g).start()

    def project(xn):
        p = jnp.dot(xn, wcur[...], preferred_element_type=F32)
        return [p[:, g * MIX_C:(g + 1) * MIX_C] for g in range(n_groups)]

    @pl.when(r < n_prompt_tiles)
    def _():
        @pl.when(lax.rem(r, tiles_per_seq) == 0)
        def _():
            abuf[0:A_HALO, :] = jnp.zeros((A_HALO, MIX_C), F32)
            bbuf[0:B_HALO, :] = jnp.zeros((B_HALO, MIX_C), F32)

        pa, ga, bg, cg, hb = project(xn_ref[...])
        abuf[A_HALO:A_HALO + MIX_T, :] = pa * jax.nn.sigmoid(ga)
        bbuf[B_HALO:B_HALO + MIX_T, :] = cg * hb

        for rs, cs, acc in _conv_a(abuf, sbuf, wa_ref, MIX_T):
            ac_ref[rs, cs] = acc + ba_ref[:, cs]

        b0 = B_HALO - (K_B - 1)
        b_c = wb_ref[0:1, :] * bbuf[b0:b0 + MIX_T, :]
        for k in range(1, K_B):
            b_c = b_c + wb_ref[k:k + 1, :] * bbuf[b0 + k:b0 + k + MIX_T, :]
        hb_ref[...] = (bg * b_c).astype(BF16)

        @pl.when(lax.rem(r, tiles_per_seq) == tiles_per_seq - 1)
        def _():
            nap_ref[0] = abuf[MIX_T + A_HALO - (K_A - 1):MIX_T + A_HALO, :]
            nbp_ref[0] = bbuf[MIX_T + b0:MIX_T + B_HALO, :]

        abuf[0:A_HALO, :] = abuf[MIX_T:MIX_T + A_HALO, :]
        bbuf[0:B_HALO, :] = bbuf[MIX_T:MIX_T + B_HALO, :]

    @pl.when(r >= n_prompt_tiles)
    def _():
        rs = pl.ds(pl.multiple_of((r - n_prompt_tiles) * SAMPLE_BS, SAMPLE_BS), SAMPLE_BS)
        pa, ga, bg, cg, hb = project(xn_ref[rs, :])
        a = pa * jax.nn.sigmoid(ga)
        acc = wa_ref[K_A - 1:K_A, :] * a
        for k in range(K_A - 1):
            acc = acc + wa_ref[k:k + 1, :] * sa_ref[k]
        ac_ref[rs, :] = acc + ba_ref[...]
        for k in range(K_A - 2):
            nas_ref[k] = sa_ref[k + 1]
        nas_ref[K_A - 2] = a

        ch = cg * hb
        b_c = wb_ref[K_B - 1:K_B, :] * ch
        for k in range(K_B - 1):
            b_c = b_c + wb_ref[k:k + 1, :] * sb_ref[:, k, :]
        hb_ref[rs, :] = (bg * b_c).astype(BF16)
        for k in range(K_B - 2):
            nbs_ref[:, k, :] = sb_ref[:, k + 1, :]
        nbs_ref[:, K_B - 2, :] = ch


def _mixer_in(xn, w_in, wa, ba, wb, sa, sb, layer, batch, seq):
    rows = xn.shape[0]
    n_sample = sa.shape[2]
    n_prompt_tiles = batch * seq // MIX_T
    tiles_per_seq = seq // MIX_T
    assert rows == batch * seq + n_sample and n_sample <= MIX_T and seq % MIX_T == 0
    assert n_sample % SAMPLE_BS == 0
    assert n_prompt_tiles + n_sample // SAMPLE_BS >= 7
    row_blk = lambda r: jnp.minimum(r, n_prompt_tiles)
    smp_blk = lambda r: jnp.maximum(r - n_prompt_tiles, 0)
    seq_of = lambda r: jnp.minimum(r, n_prompt_tiles - 1) // tiles_per_seq
    chan = lambda k: pl.BlockSpec((k, MIX_C), lambda c, r: (0, c))
    state = lambda k, lyr: pl.BlockSpec((None, SAMPLE_BS, k, MIX_C),
                                        lambda c, r: (lyr, smp_blk(r), 0, c))
    tap_major = lambda k, lyr: pl.BlockSpec((None, k, SAMPLE_BS, MIX_C),
                                            lambda c, r: (lyr, 0, smp_blk(r), c))
    prompt_state = lambda k: pl.BlockSpec((None, 1, k, MIX_C), lambda c, r: (0, seq_of(r), 0, c))
    return pl.pallas_call(
        functools.partial(_mixer_in_kernel, n_prompt_tiles=n_prompt_tiles,
                          tiles_per_seq=tiles_per_seq),
        out_shape=(jax.ShapeDtypeStruct((rows, D_A), F32),
                   jax.ShapeDtypeStruct((rows, D_B), BF16),
                   jax.ShapeDtypeStruct((1, batch, K_A - 1, D_A), F32),
                   jax.ShapeDtypeStruct((1, batch, K_B - 1, D_B), F32),
                   jax.ShapeDtypeStruct((1, K_A - 1, n_sample, D_A), F32),
                   jax.ShapeDtypeStruct((1, n_sample, K_B - 1, D_B), F32)),
        grid=(D_A // MIX_C, n_prompt_tiles + n_sample // SAMPLE_BS),
        in_specs=[pl.BlockSpec((MIX_T, D_MODEL), lambda c, r: (row_blk(r), 0)),
                  pl.BlockSpec(memory_space=pl.ANY),
                  chan(K_A), chan(1), chan(K_B),
                  tap_major(K_A - 1, layer), state(K_B - 1, layer)],
        out_specs=(pl.BlockSpec((MIX_T, MIX_C), lambda c, r: (row_blk(r), c)),
                   pl.BlockSpec((MIX_T, MIX_C), lambda c, r: (row_blk(r), c)),
                   prompt_state(K_A - 1), prompt_state(K_B - 1),
                   tap_major(K_A - 1, 0), state(K_B - 1, 0)),
        scratch_shapes=[pltpu.VMEM((2, D_MODEL, 5 * MIX_C), BF16),
                        pltpu.VMEM((A_HALO + MIX_T, MIX_C), F32),
                        pltpu.VMEM((SUBLANES - 1, A_HALO + MIX_T - SUBLANES, MIX_C), F32),
                        pltpu.VMEM((B_HALO + MIX_T, MIX_C), F32),
                        pltpu.VMEM((2, D_MODEL, MIX_C), F32),
                        pltpu.SemaphoreType.DMA((2,))],
        compiler_params=_params(("arbitrary", "arbitrary")),
        name="mixer_in",
    )(xn, w_in, wa, ba, wb, sa, sb)


def _mixer_out_kernel(ac_ref, hb_ref, lg_ref, lb_ref, w_hbm, x_ref, o_ref, wb_ref, stage, sem):
    j = pl.program_id(0)

    @pl.when(pl.program_id(1) == 0)
    def _():
        col = pl.multiple_of(j * TN, TN)
        offs = range(0, TN, W_STAGE)
        _load_bf16_columns(w_hbm, [col + o for o in offs], W_STAGE,
                           [wb_ref.at[:, pl.ds(o, W_STAGE)] for o in offs], stage, sem)

    a_ln = _layernorm_rows(ac_ref[...], lg_ref[...], lb_ref[...])
    h_a = (a_ln * jax.nn.sigmoid(a_ln)).astype(BF16)
    y = jnp.dot(h_a, wb_ref[0:D_A, :], preferred_element_type=F32)
    y = y + jnp.dot(hb_ref[...], wb_ref[D_A:D_A + D_B, :], preferred_element_type=F32)
    o_ref[...] = x_ref[...] + y


def _mixer_out(ac, hb, lg, lb, w, x):
    rows = x.shape[0]
    return pl.pallas_call(
        _mixer_out_kernel,
        out_shape=jax.ShapeDtypeStruct((rows, D_MODEL), F32),
        grid=(D_MODEL // TN, rows // TM),
        in_specs=[
            pl.BlockSpec((TM, D_A), lambda j, i: (i, 0)),
            pl.BlockSpec((TM, D_B), lambda j, i: (i, 0)),
            pl.BlockSpec((1, D_A), lambda j, i: (0, 0)),
            pl.BlockSpec((1, D_A), lambda j, i: (0, 0)),
            pl.BlockSpec(memory_space=pl.ANY),
            pl.BlockSpec((TM, TN), lambda j, i: (i, j)),
        ],
        out_specs=pl.BlockSpec((TM, TN), lambda j, i: (i, j)),
        scratch_shapes=[pltpu.VMEM((D_A + D_B, TN), BF16),
                        pltpu.VMEM((2, D_A + D_B, W_STAGE), F32),
                        pltpu.SemaphoreType.DMA((2,))],
        compiler_params=_params(("arbitrary", "arbitrary")),
        name="mixer_out",
    )(ac, hb, lg, lb, w, x)


def _gate_kernel(z_ref, lg_ref, lb_ref, ws_ref, bs_ref, o_ref, v_ref, *, n_prompt_chunks):
    i = pl.program_id(0)
    row = lax.broadcasted_iota(jnp.int32, (CHUNK, CHUNK), 0)
    col = lax.broadcasted_iota(jnp.int32, (CHUNK, CHUNK), 1)
    for c in range(GATE_CHUNKS):
        rs = slice(c * CHUNK, (c + 1) * CHUNK)
        is_sample = i * GATE_CHUNKS + c >= n_prompt_chunks
        sel = jnp.where(is_sample, 1, 0)
        vn = _layernorm_rows(z_ref[rs, D_C:2 * D_C], lg_ref[...], lb_ref[...])
        vb = vn.astype(BF16)
        keep = (col <= row) & (col >= jnp.where(is_sample, row, 0))
        for h in range(H_C):
            hs = slice(h * DH_C, (h + 1) * DH_C)
            w = jnp.where(keep, ws_ref[sel, h], 0.0).astype(BF16)
            s = jnp.dot(w, vb[:, hs], preferred_element_type=F32) + bs_ref[sel][:, h:h + 1]
            o_ref[rs, hs] = (z_ref[rs, hs] * s).astype(BF16)

        if c == GATE_CHUNKS - 1:
            @pl.when(i == pl.num_programs(0) - 1)
            def _():
                v_ref[...] = vn


def _gate(z, lg, lb, ws_all, bs_all, n_prompt_chunks):
    rows = z.shape[0]
    tile = GATE_CHUNKS * CHUNK
    assert rows % tile == 0 and rows // CHUNK == n_prompt_chunks + 1
    return pl.pallas_call(
        functools.partial(_gate_kernel, n_prompt_chunks=n_prompt_chunks),
        out_shape=(jax.ShapeDtypeStruct((rows, D_C), BF16),
                   jax.ShapeDtypeStruct((CHUNK, D_C), F32)),
        grid=(rows // tile,),
        in_specs=[
            pl.BlockSpec((tile, 2 * D_C), lambda i: (i, 0)),
            pl.BlockSpec((1, D_C), lambda i: (0, 0)),
            pl.BlockSpec((1, D_C), lambda i: (0, 0)),
            pl.BlockSpec((2, H_C, CHUNK, CHUNK), lambda i: (0, 0, 0, 0)),
            pl.BlockSpec((2, CHUNK, H_C), lambda i: (0, 0, 0)),
        ],
        out_specs=(pl.BlockSpec((tile, D_C), lambda i: (i, 0)),
                   pl.BlockSpec((CHUNK, D_C), lambda i: (0, 0))),
        compiler_params=_params(("arbitrary",)),
        name="gate",
    )(z, lg, lb, ws_all, bs_all)


def kernel(x_prompt, x_sample, state_conv_a, state_conv_b, norm_g, ffn_w1, ffn_w3, ffn_w2,
           ab_w_in, a_conv_w, a_conv_b, a_ln_g, a_ln_b, b_conv_w, ab_w_out,
           c_w_in, c_b_in, c_ln_g, c_ln_b, c_w_s, c_b_s, c_w_out, final_g):
    batch, seq, _ = x_prompt.shape
    nb, dec_seq, _ = x_sample.shape
    depth = norm_g.shape[0]
    n_prompt = batch * seq
    rows = n_prompt + nb
    assert dec_seq == 1 and nb == CHUNK and n_prompt % CHUNK == 0

    row = lambda v: v.reshape(1, -1)
    ffn_stacked = (ffn_w1, ffn_w3, ffn_w2)
    w_bf16 = None

    xs =[x_prompt.reshape(n_prompt, D_MODEL), x_sample.reshape(nb, D_MODEL)]
    state_a_taps = jnp.swapaxes(state_conv_a, 1, 2)
    new_a_p, new_b_p, new_a_s, new_b_s, new_v_s = [], [], [], [], []
    for i in range(depth):
        j = i // 2
        head = None
        if w_bf16 is None:
            (x_head,), xn_head, w_bf16, _ = _ffn(
                xs, row(norm_g[i, 0]), None, row(norm_g[i, 1]), (TM,), tf=TF_F32, n_tiles=1,
                own_f32=(ffn_stacked, (i, 0)), emit_norm=True)
            head = (x_head, xn_head)
        (x,), xn, _, w_bf16 = _ffn(xs, row(norm_g[i, 0]), w_bf16, row(norm_g[i, 1]), (rows,),
                                   tile0=0 if head is None else 1, head=head,
                                   next_w=(ffn_stacked, (i, 1)), emit_norm=True)
        if i % 2 == 0:
            ac, hb, sa_p, sb_p, sa_s, sb_s = _mixer_in(
                xn, ab_w_in[j], a_conv_w[j], row(a_conv_b[j]), b_conv_w[j],
                state_a_taps, state_conv_b, j, batch, seq)
            new_a_p.append(sa_p)
            new_b_p.append(sb_p)
            new_a_s.append(jnp.swapaxes(sa_s, 1, 2))
            new_b_s.append(sb_s)
            x = _mixer_out(ac, hb, row(a_ln_g[j]), row(a_ln_b[j]), ab_w_out[j], x)
        else:
            z = _proj(xn, c_w_in[j], row(c_b_in[j]), True)
            w_s = c_w_s[j][:, :CHUNK, :CHUNK]
            ws_all = jnp.stack([w_s, jnp.broadcast_to(w_s[:, :1, :1], w_s.shape)])
            b_s = c_b_s[j][:, :CHUNK].T
            bs_all = jnp.stack([b_s, jnp.broadcast_to(b_s[:1], b_s.shape)])
            h, v_s = _gate(z, row(c_ln_g[j]), row(c_ln_b[j]), ws_all, bs_all, n_prompt // CHUNK)
            new_v_s.append(v_s.reshape(nb, 1, D_C))
            x = _proj_residual(h, c_w_out[j], x)
        last = i == depth - 1
        xs, _, _, w_bf16 = _ffn([x], row(norm_g[i, 2]), w_bf16, row(final_g),
                                (n_prompt, nb) if last else (rows,),
                                next_w=None if last else (ffn_stacked, (i + 1, 0)),
                                final_norm=last)

    y_prompt, y_sample = xs
    return (y_prompt.reshape(batch, seq, D_MODEL), y_sample.reshape(nb, 1, D_MODEL),
            jnp.concatenate(new_a_p), jnp.concatenate(new_b_p),
            jnp.concatenate(new_a_s), jnp.concatenate(new_b_s),
            jnp.stack(new_v_s))
```

```python
import functools

import jax
import jax.numpy as jnp
from jax import lax
from jax.experimental import pallas as pl
from jax.experimental.pallas import tpu as pltpu

F32 = jnp.float32
BF16 = jnp.bfloat16

D_MODEL = 2048
D_FF = 5632
D_A = 1024
D_B = 1024
K_A = 31
K_B = 3
D_C = 2048
H_C = 8
DH_C = D_C // H_C
CHUNK = 128
EPS = 1e-6

VMEM_LIMIT_BYTES = 56 * 1024 * 1024
LANES = 128
SUBLANES = 8

TM = 1040
NORM_ROWS = 208
TF = 512
TF_F32 = 256
DOWN_CHUNK = 512
TN = 1024
W_STAGE = 256
MIX_T = 1024
MIX_C = 256
SAMPLE_BS = 128
CONV_RB = 64
A_HALO = 32
B_HALO = 8
GATE_CHUNKS = 5


def _params(sem):
    return pltpu.CompilerParams(dimension_semantics=sem, vmem_limit_bytes=VMEM_LIMIT_BYTES)


def _rms_rows(x, g):
    ms = jnp.mean(x * x, axis=-1, keepdims=True)
    return x * lax.rsqrt(ms + EPS) * g


def _layernorm_rows(x, g, b):
    mu = jnp.mean(x, axis=-1, keepdims=True)
    xc = x - mu
    var = jnp.mean(xc * xc, axis=-1, keepdims=True)
    return xc * lax.rsqrt(var + EPS) * g + b


def _norm_chunks(src_ref, g_ref, dst_ref, dtype):
    def body(c, carry):
        r = pl.ds(pl.multiple_of(c * NORM_ROWS, NORM_ROWS), NORM_ROWS)
        dst_ref[r, :] = _rms_rows(src_ref[r, :], g_ref[...]).astype(dtype)
        return carry
    lax.fori_loop(0, TM // NORM_ROWS, body, 0)


def _tile_pieces(tile, seg_rows):
    lo, hi = tile * TM, (tile + 1) * TM
    pieces, base = [], 0
    for k, n in enumerate(seg_rows):
        a, b = max(lo, base), min(hi, base + n)
        if a < b:
            pieces.append((k, a - base, a - lo, b - a))
        base += n
    return pieces


def _ffn_kernel(*refs, in_rows, out_rows, tile0, n_tiles, own_f32, convert_next, emit_norm,
                final_norm, head_rows):
    n_in, n_out = len(in_rows), len(out_rows)
    it = iter(refs)
    x_srcs = [next(it) for _ in range(n_in)]
    g_ref, w1_ref, w3_ref, w2_ref, post_g_ref = (next(it) for _ in range(5))
    next_w = [next(it) for _ in range(3)] if convert_next else []
    head_srcs = [next(it) for _ in range(2)] if head_rows else []
    o_dsts = [next(it) for _ in range(n_out)]
    xn_out = next(it) if emit_norm else None
    own_w_out = [next(it) for _ in range(3)] if own_f32 else []
    next_w_out = [next(it) for _ in range(3)] if convert_next else []
    acc_ref, xn_ref, load_sem, store_sem = (next(it) for _ in range(4))
    xn_next_ref, xn_sem = (next(it), next(it)) if emit_norm else (None, None)
    head_sem = next(it) if head_rows else None

    i = pl.program_id(0)
    j = pl.program_id(1)
    n_i = pl.num_programs(0)
    n_j = pl.num_programs(1)
    tile = i + tile0

    def load_copies(t):
        return [pltpu.make_async_copy(x_srcs[k].at[pl.ds(r0, n)],
                                      acc_ref.at[t % 2, pl.ds(d0, n)], load_sem.at[t % 2])
                for k, r0, d0, n in _tile_pieces(t, in_rows)]

    def store_copies(t):
        return [pltpu.make_async_copy(acc_ref.at[t % 2, pl.ds(d0, n)],
                                      o_dsts[k].at[pl.ds(r0, n)], store_sem.at[t % 2])
                for k, r0, d0, n in _tile_pieces(t, out_rows)]

    def xn_copy(t):
        return pltpu.make_async_copy(xn_next_ref, xn_out.at[pl.ds(t * TM, TM)], xn_sem)

    def head_copies():
        return [pltpu.make_async_copy(head_srcs[0], o_dsts[0].at[pl.ds(0, head_rows)], head_sem.at[0]),
                pltpu.make_async_copy(head_srcs[1], xn_out.at[pl.ds(0, head_rows)], head_sem.at[1])]

    def on_tile(which, fn):
        for t in range(tile0, tile0 + n_tiles):
            @pl.when(which == t)
            def _(t=t):
                fn(t)

    acc = acc_ref.at[lax.rem(tile, 2)]

    def matmuls():
        if own_f32:
            w1, w3, w2 = (w[...].astype(BF16) for w in (w1_ref, w3_ref, w2_ref))
            for dst, w in zip(own_w_out, (w1, w3, w2)):
                dst[...] = w
            w2_cols = lambda cs: w2[:, cs]
        else:
            w1, w3 = w1_ref[...], w3_ref[...]
            w2_cols = lambda cs: w2_ref[:, cs]
        xn = xn_ref[...]
        h1 = jnp.dot(xn, w1, preferred_element_type=F32)
        h3 = jnp.dot(xn, w3, preferred_element_type=F32)
        h = (0.5 * (h1 * jax.nn.sigmoid(h1)) * h3).astype(BF16)
        for n in range(D_MODEL // DOWN_CHUNK):
            cs = slice(n * DOWN_CHUNK, (n + 1) * DOWN_CHUNK)
            acc[:, cs] += jnp.dot(h, w2_cols(cs), preferred_element_type=F32)

    @pl.when(j == 0)
    def _():
        @pl.when(i == 0)
        def _():
            for cp in load_copies(tile0):
                cp.start()
            if head_rows:
                for cp in head_copies():
                    cp.start()
        on_tile(tile, lambda t: [cp.wait() for cp in load_copies(t)])
        for r0 in range(0, TM, NORM_ROWS):
            xn_ref[r0:r0 + NORM_ROWS, :] = _rms_rows(acc[r0:r0 + NORM_ROWS, :],
                                                     g_ref[...]).astype(BF16)
        matmuls()

    @pl.when(j == 2)
    def _():
        @pl.when(i > 0)
        def _():
            on_tile(tile - 1, lambda t: [cp.wait() for cp in store_copies(t)])
            if emit_norm:
                on_tile(tile - 1, lambda t: xn_copy(t).wait())

        @pl.when(i + 1 < n_i)
        def _():
            on_tile(tile + 1, lambda t: [cp.start() for cp in load_copies(t)])

    @pl.when(j > 0)
    def _():
        matmuls()

    if convert_next:
        for src, dst in zip(next_w, next_w_out):
            dst[...] = src[...].astype(BF16)

    @pl.when(j == n_j - 1)
    def _():
        if final_norm:
            _norm_chunks(acc, post_g_ref, acc, F32)
        on_tile(tile, lambda t: [cp.start() for cp in store_copies(t)])
        if emit_norm:
            _norm_chunks(acc, post_g_ref, xn_next_ref, BF16)
            on_tile(tile, lambda t: xn_copy(t).start())

        @pl.when(i == n_i - 1)
        def _():
            on_tile(tile, lambda t: [cp.wait() for cp in store_copies(t)])
            if emit_norm:
                on_tile(tile, lambda t: xn_copy(t).wait())
            if head_rows:
                for cp in head_copies():
                    cp.wait()


def _round_up(n, m):
    return -(-n // m) * m


def _ffn(xs, g, w_bf16, post_g, out_rows, *, tf=TF, tile0=0, n_tiles=None, own_f32=None,
         next_w=None, emit_norm=False, final_norm=False, head=None):
    in_rows = tuple(a.shape[0] for a in xs)
    rows = sum(in_rows)
    assert rows % TM == 0 and D_FF % tf == 0 and D_MODEL % DOWN_CHUNK == 0
    n_i = rows // TM - tile0 if n_tiles is None else n_tiles
    n_j = D_FF // tf
    out_total = sum(out_rows)
    assert n_j > 2 and out_total == (tile0 + n_i) * TM
    head_rows = 0 if head is None else head[0].shape[0]
    assert head_rows == (tile0 * TM if head is not None else 0) and (emit_norm or head is None)
    convert_next = next_w is not None
    any_spec = pl.BlockSpec(memory_space=pl.ANY)
    row_spec = pl.BlockSpec((1, D_MODEL), lambda i, j: (0, 0))

    if own_f32 is None:
        w_specs = [pl.BlockSpec((D_MODEL, tf), lambda i, j: (0, j)),
                   pl.BlockSpec((D_MODEL, tf), lambda i, j: (0, j)),
                   pl.BlockSpec((tf, D_MODEL), lambda i, j: (j, 0))]
        w_args = list(w_bf16)
    else:
        own_stacked, (oli, osi) = own_f32
        w_specs = [pl.BlockSpec((None, None, D_MODEL, tf), lambda i, j: (oli, osi, 0, j)),
                   pl.BlockSpec((None, None, D_MODEL, tf), lambda i, j: (oli, osi, 0, j)),
                   pl.BlockSpec((None, None, tf, D_MODEL), lambda i, j: (oli, osi, j, 0))]
        w_args = list(own_stacked)

    in_specs = [any_spec] * len(xs) + [row_spec] + w_specs + [row_spec]
    args = list(xs) + [g] + w_args + [post_g]
    out_shape = [jax.ShapeDtypeStruct((n, D_MODEL), F32) for n in out_rows]
    out_specs = [any_spec] * len(out_rows)
    if convert_next:
        stacked, (li, si) = next_w
        pr = _round_up(pl.cdiv(D_MODEL, n_i), 2 * SUBLANES)
        pd = _round_up(pl.cdiv(D_FF, n_i * n_j), 2 * SUBLANES)
        last_pd = pl.cdiv(D_FF, pd) - 1
        down_idx = lambda i, j: jnp.minimum(i * n_j + j, last_pd)
        assert (n_i - 1) * pr < D_MODEL
        in_specs += [
            pl.BlockSpec((None, None, pr, tf), lambda i, j: (li, si, i, j)),
            pl.BlockSpec((None, None, pr, tf), lambda i, j: (li, si, i, j)),
            pl.BlockSpec((None, None, pd, D_MODEL), lambda i, j: (li, si, down_idx(i, j), 0)),
        ]
        args += list(stacked)
    if head is not None:
        in_specs += [any_spec, any_spec]
        args += list(head)
    if emit_norm:
        out_shape.append(jax.ShapeDtypeStruct((out_total, D_MODEL), BF16))
        out_specs.append(any_spec)
    if own_f32 is not None:
        out_shape += [jax.ShapeDtypeStruct(w.shape[2:], BF16) for w in own_stacked]
        out_specs += [pl.BlockSpec((D_MODEL, tf), lambda i, j: (0, j)),
                      pl.BlockSpec((D_MODEL, tf), lambda i, j: (0, j)),
                      pl.BlockSpec((tf, D_MODEL), lambda i, j: (j, 0))]
    if convert_next:
        out_shape += [jax.ShapeDtypeStruct(w.shape[2:], BF16) for w in stacked]
        out_specs += [pl.BlockSpec((pr, tf), lambda i, j: (i, j)),
                      pl.BlockSpec((pr, tf), lambda i, j: (i, j)),
                      pl.BlockSpec((pd, D_MODEL), lambda i, j: (down_idx(i, j), 0))]
    scratch = [pltpu.VMEM((2, TM, D_MODEL), F32), pltpu.VMEM((TM, D_MODEL), BF16),
               pltpu.SemaphoreType.DMA((2,)), pltpu.SemaphoreType.DMA((2,))]
    if emit_norm:
        scratch += [pltpu.VMEM((TM, D_MODEL), BF16), pltpu.SemaphoreType.DMA(())]
    if head is not None:
        scratch += [pltpu.SemaphoreType.DMA((2,))]

    res = pl.pallas_call(
        functools.partial(_ffn_kernel, in_rows=in_rows, out_rows=tuple(out_rows), tile0=tile0,
                          n_tiles=n_i, own_f32=own_f32 is not None, convert_next=convert_next,
                          emit_norm=emit_norm, final_norm=final_norm, head_rows=head_rows),
        out_shape=out_shape,
        grid=(n_i, n_j),
        in_specs=in_specs,
        out_specs=out_specs,
        scratch_shapes=scratch,
        compiler_params=_params(("arbitrary", "arbitrary")),
        name="ffn",
    )(*args)
    res = list(res)
    outs = [res.pop(0) for _ in out_rows]
    xn = res.pop(0) if emit_norm else None
    own = [res.pop(0) for _ in range(3)] if own_f32 is not None else None
    return outs, xn, own, (res if convert_next else None)


def _proj_kernel(a_ref, w_ref, b_ref, o_ref, wb_ref, *, gelu):
    @pl.when(pl.program_id(1) == 0)
    def _():
        wb_ref[...] = w_ref[...].astype(BF16)

    y = jnp.dot(a_ref[...], wb_ref[...], preferred_element_type=F32)
    if gelu:
        y = y + b_ref[...]
        y = 0.5 * y * (1.0 + lax.erf(y * (2.0 ** -0.5)))
    o_ref[...] = y


def _proj(a, w, b, gelu):
    rows, k = a.shape
    n = w.shape[1]
    return pl.pallas_call(
        functools.partial(_proj_kernel, gelu=gelu),
        out_shape=jax.ShapeDtypeStruct((rows, n), F32),
        grid=(n // TN, rows // TM),
        in_specs=[
            pl.BlockSpec((TM, k), lambda j, i: (i, 0)),
            pl.BlockSpec((k, TN), lambda j, i: (0, j)),
            pl.BlockSpec((1, TN), lambda j, i: (0, j)),
        ],
        out_specs=pl.BlockSpec((TM, TN), lambda j, i: (i, j)),
        scratch_shapes=[pltpu.VMEM((k, TN), BF16)],
        compiler_params=_params(("arbitrary", "arbitrary")),
        name="proj_gelu" if gelu else "proj",
    )(a, w, b)


def _proj_residual_kernel(h_ref, w_ref, x_ref, o_ref, wb_ref):
    @pl.when(pl.program_id(1) == 0)
    def _():
        wb_ref[...] = w_ref[...].astype(BF16)

    o_ref[...] = x_ref[...] + jnp.dot(h_ref[...], wb_ref[...], preferred_element_type=F32)


def _proj_residual(h, w, x):
    rows, k = h.shape
    return pl.pallas_call(
        _proj_residual_kernel,
        out_shape=jax.ShapeDtypeStruct((rows, D_MODEL), F32),
        grid=(D_MODEL // TN, rows // TM),
        in_specs=[
            pl.BlockSpec((TM, k), lambda j, i: (i, 0)),
            pl.BlockSpec((k, TN), lambda j, i: (0, j)),
            pl.BlockSpec((TM, TN), lambda j, i: (i, j)),
        ],
        out_specs=pl.BlockSpec((TM, TN), lambda j, i: (i, j)),
        scratch_shapes=[pltpu.VMEM((k, TN), BF16)],
        compiler_params=_params(("arbitrary", "arbitrary")),
        name="proj_residual",
    )(h, w, x)


def _conv_a(abuf, sbuf, wa_ref, n_rows):
    shift_rows = A_HALO + n_rows - SUBLANES
    for s in range(1, SUBLANES):
        sbuf[s - 1, 0:shift_rows, :] = abuf[s:s + shift_rows, :]

    a0 = A_HALO - (K_A - 1)

    def tap(k, r0, cs):
        q, s = divmod(a0 + k, SUBLANES)
        src = abuf if s == 0 else sbuf.at[s - 1]
        return wa_ref[k:k + 1, cs] * src[SUBLANES * q + r0:SUBLANES * q + r0 + CONV_RB, cs]

    for c in range(MIX_C // LANES):
        cs = slice(c * LANES, (c + 1) * LANES)
        for r0 in range(0, n_rows, CONV_RB):
            acc = tap(0, r0, cs)
            for k in range(1, K_A):
                acc = acc + tap(k, r0, cs)
            yield slice(r0, r0 + CONV_RB), cs, acc


def _load_bf16_columns(w_hbm, col_starts, width, dsts, stage, sem):
    def chunk(c):
        return pltpu.make_async_copy(w_hbm.at[:, pl.ds(col_starts[c], width)], stage.at[c % 2],
                                     sem.at[c % 2])
    chunk(0).start()
    for c in range(len(col_starts)):
        if c + 1 < len(col_starts):
            chunk(c + 1).start()
        chunk(c).wait()
        dsts[c][...] = stage[c % 2].astype(BF16)


def _mixer_in_kernel(xn_ref, w_hbm, wa_ref, ba_ref, wb_ref, sa_ref, sb_ref,
                     ac_ref, hb_ref, nap_ref, nbp_ref, nas_ref, nbs_ref,
                     wbf, abuf, sbuf, bbuf, stage, sem, *, n_prompt_tiles, tiles_per_seq):
    c_tile = pl.program_id(0)
    r = pl.program_id(1)
    n_groups = 5
    wcur = wbf.at[lax.rem(c_tile, 2)]
    wnext = wbf.at[1 - lax.rem(c_tile, 2)]

    @pl.when((c_tile == 0) & (r == 0))
    def _():
        _load_bf16_columns(w_hbm, [g * D_A for g in range(n_groups)], MIX_C,
                           [wcur.at[:, pl.ds(g * MIX_C, MIX_C)] for g in range(n_groups)],
                           stage, sem)

    @pl.when(c_tile + 1 < pl.num_programs(0))
    def _():
        col = pl.multiple_of((c_tile + 1) * MIX_C, MIX_C)

        def fetch(g):
            return pltpu.make_async_copy(w_hbm.at[:, pl.ds(g * D_A + col, MIX_C)],
                                         stage.at[g % 2], sem.at[g % 2])
        for g in range(n_groups):
            @pl.when(r == g + 2)
            def _(g=g):
                fetch(g).wait()
                wnext[:, g * MIX_C:(g + 1) * MIX_C] = stage[g % 2].astype(BF16)

            @pl.when(r == g + 1)
            def _(g=g):
                fetch(g).start()

    def project(xn):
        p = jnp.dot(xn, wcur[...], preferred_element_type=F32)
        return [p[:, g * MIX_C:(g + 1) * MIX_C] for g in range(n_groups)]

    @pl.when(r < n_prompt_tiles)
    def _():
        @pl.when(lax.rem(r, tiles_per_seq) == 0)
        def _():
            abuf[0:A_HALO, :] = jnp.zeros((A_HALO, MIX_C), F32)
            bbuf[0:B_HALO, :] = jnp.zeros((B_HALO, MIX_C), F32)

        pa, ga, bg, cg, hb = project(xn_ref[...])
        abuf[A_HALO:A_HALO + MIX_T, :] = pa * jax.nn.sigmoid(ga)
        bbuf[B_HALO:B_HALO + MIX_T, :] = cg * hb

        for rs, cs, acc in _conv_a(abuf, sbuf, wa_ref, MIX_T):
            ac_ref[rs, cs] = acc + ba_ref[:, cs]

        b0 = B_HALO - (K_B - 1)
        b_c = wb_ref[0:1, :] * bbuf[b0:b0 + MIX_T, :]
        for k in range(1, K_B):
            b_c = b_c + wb_ref[k:k + 1, :] * bbuf[b0 + k:b0 + k + MIX_T, :]
        hb_ref[...] = (bg * b_c).astype(BF16)

        @pl.when(lax.rem(r, tiles_per_seq) == tiles_per_seq - 1)
        def _():
            nap_ref[0] = abuf[MIX_T + A_HALO - (K_A - 1):MIX_T + A_HALO, :]
            nbp_ref[0] = bbuf[MIX_T + b0:MIX_T + B_HALO, :]

        abuf[0:A_HALO, :] = abuf[MIX_T:MIX_T + A_HALO, :]
        bbuf[0:B_HALO, :] = bbuf[MIX_T:MIX_T + B_HALO, :]

    @pl.when(r >= n_prompt_tiles)
    def _():
        rs = pl.ds(pl.multiple_of((r - n_prompt_tiles) * SAMPLE_BS, SAMPLE_BS), SAMPLE_BS)
        pa, ga, bg, cg, hb = project(xn_ref[rs, :])
        a = pa * jax.nn.sigmoid(ga)
        acc = wa_ref[K_A - 1:K_A, :] * a
        for k in range(K_A - 1):
            acc = acc + wa_ref[k:k + 1, :] * sa_ref[k]
        ac_ref[rs, :] = acc + ba_ref[...]
        for k in range(K_A - 2):
            nas_ref[k] = sa_ref[k + 1]
        nas_ref[K_A - 2] = a

        ch = cg * hb
        b_c = wb_ref[K_B - 1:K_B, :] * ch
        for k in range(K_B - 1):
            b_c = b_c + wb_ref[k:k + 1, :] * sb_ref[:, k, :]
        hb_ref[rs, :] = (bg * b_c).astype(BF16)
        for k in range(K_B - 2):
            nbs_ref[:, k, :] = sb_ref[:, k + 1, :]
        nbs_ref[:, K_B - 2, :] = ch


def _mixer_in(xn, w_in, wa, ba, wb, sa, sb, layer, batch, seq):
    rows = xn.shape[0]
    n_sample = sa.shape[2]
    n_prompt_tiles = batch * seq // MIX_T
    tiles_per_seq = seq // MIX_T
    assert rows == batch * seq + n_sample and n_sample <= MIX_T and seq % MIX_T == 0
    assert n_sample % SAMPLE_BS == 0
    assert n_prompt_tiles + n_sample // SAMPLE_BS >= 7
    row_blk = lambda r: jnp.minimum(r, n_prompt_tiles)
    smp_blk = lambda r: jnp.maximum(r - n_prompt_tiles, 0)
    seq_of = lambda r: jnp.minimum(r, n_prompt_tiles - 1) // tiles_per_seq
    chan = lambda k: pl.BlockSpec((k, MIX_C), lambda c, r: (0, c))
    state = lambda k, lyr: pl.BlockSpec((None, SAMPLE_BS, k, MIX_C),
                                        lambda c, r: (lyr, smp_blk(r), 0, c))
    tap_major = lambda k, lyr: pl.BlockSpec((None, k, SAMPLE_BS, MIX_C),
                                            lambda c, r: (lyr, 0, smp_blk(r), c))
    prompt_state = lambda k: pl.BlockSpec((None, 1, k, MIX_C), lambda c, r: (0, seq_of(r), 0, c))
    return pl.pallas_call(
        functools.partial(_mixer_in_kernel, n_prompt_tiles=n_prompt_tiles,
                          tiles_per_seq=tiles_per_seq),
        out_shape=(jax.ShapeDtypeStruct((rows, D_A), F32),
                   jax.ShapeDtypeStruct((rows, D_B), BF16),
                   jax.ShapeDtypeStruct((1, batch, K_A - 1, D_A), F32),
                   jax.ShapeDtypeStruct((1, batch, K_B - 1, D_B), F32),
                   jax.ShapeDtypeStruct((1, K_A - 1, n_sample, D_A), F32),
                   jax.ShapeDtypeStruct((1, n_sample, K_B - 1, D_B), F32)),
        grid=(D_A // MIX_C, n_prompt_tiles + n_sample // SAMPLE_BS),
        in_specs=[pl.BlockSpec((MIX_T, D_MODEL), lambda c, r: (row_blk(r), 0)),
                  pl.BlockSpec(memory_space=pl.ANY),
                  chan(K_A), chan(1), chan(K_B),
                  tap_major(K_A - 1, layer), state(K_B - 1, layer)],
        out_specs=(pl.BlockSpec((MIX_T, MIX_C), lambda c, r: (row_blk(r), c)),
                   pl.BlockSpec((MIX_T, MIX_C), lambda c, r: (row_blk(r), c)),
                   prompt_state(K_A - 1), prompt_state(K_B - 1),
                   tap_major(K_A - 1, 0), state(K_B - 1, 0)),
        scratch_shapes=[pltpu.VMEM((2, D_MODEL, 5 * MIX_C), BF16),
                        pltpu.VMEM((A_HALO + MIX_T, MIX_C), F32),
                        pltpu.VMEM((SUBLANES - 1, A_HALO + MIX_T - SUBLANES, MIX_C), F32),
                        pltpu.VMEM((B_HALO + MIX_T, MIX_C), F32),
                        pltpu.VMEM((2, D_MODEL, MIX_C), F32),
                        pltpu.SemaphoreType.DMA((2,))],
        compiler_params=_params(("arbitrary", "arbitrary")),
        name="mixer_in",
    )(xn, w_in, wa, ba, wb, sa, sb)


def _mixer_out_kernel(ac_ref, hb_ref, lg_ref, lb_ref, w_hbm, x_ref, o_ref, wb_ref, stage, sem):
    j = pl.program_id(0)

    @pl.when(pl.program_id(1) == 0)
    def _():
        col = pl.multiple_of(j * TN, TN)
        offs = range(0, TN, W_STAGE)
        _load_bf16_columns(w_hbm, [col + o for o in offs], W_STAGE,
                           [wb_ref.at[:, pl.ds(o, W_STAGE)] for o in offs], stage, sem)

    a_ln = _layernorm_rows(ac_ref[...], lg_ref[...], lb_ref[...])
    h_a = (a_ln * jax.nn.sigmoid(a_ln)).astype(BF16)
    y = jnp.dot(h_a, wb_ref[0:D_A, :], preferred_element_type=F32)
    y = y + jnp.dot(hb_ref[...], wb_ref[D_A:D_A + D_B, :], preferred_element_type=F32)
    o_ref[...] = x_ref[...] + y


def _mixer_out(ac, hb, lg, lb, w, x):
    rows = x.shape[0]
    return pl.pallas_call(
        _mixer_out_kernel,
        out_shape=jax.ShapeDtypeStruct((rows, D_MODEL), F32),
        grid=(D_MODEL // TN, rows // TM),
        in_specs=[
            pl.BlockSpec((TM, D_A), lambda j, i: (i, 0)),
            pl.BlockSpec((TM, D_B), lambda j, i: (i, 0)),
            pl.BlockSpec((1, D_A), lambda j, i: (0, 0)),
            pl.BlockSpec((1, D_A), lambda j, i: (0, 0)),
            pl.BlockSpec(memory_space=pl.ANY),
            pl.BlockSpec((TM, TN), lambda j, i: (i, j)),
        ],
        out_specs=pl.BlockSpec((TM, TN), lambda j, i: (i, j)),
        scratch_shapes=[pltpu.VMEM((D_A + D_B, TN), BF16),
                        pltpu.VMEM((2, D_A + D_B, W_STAGE), F32),
                        pltpu.SemaphoreType.DMA((2,))],
        compiler_params=_params(("arbitrary", "arbitrary")),
        name="mixer_out",
    )(ac, hb, lg, lb, w, x)


def _gate_kernel(z_ref, lg_ref, lb_ref, ws_ref, bs_ref, o_ref, v_ref, *, n_prompt_chunks):
    i = pl.program_id(0)
    row = lax.broadcasted_iota(jnp.int32, (CHUNK, CHUNK), 0)
    col = lax.broadcasted_iota(jnp.int32, (CHUNK, CHUNK), 1)
    for c in range(GATE_CHUNKS):
        rs = slice(c * CHUNK, (c + 1) * CHUNK)
        is_sample = i * GATE_CHUNKS + c >= n_prompt_chunks
        sel = jnp.where(is_sample, 1, 0)
        vn = _layernorm_rows(z_ref[rs, D_C:2 * D_C], lg_ref[...], lb_ref[...])
        vb = vn.astype(BF16)
        keep = (col <= row) & (col >= jnp.where(is_sample, row, 0))
        for h in range(H_C):
            hs = slice(h * DH_C, (h + 1) * DH_C)
            w = jnp.where(keep, ws_ref[sel, h], 0.0).astype(BF16)
            s = jnp.dot(w, vb[:, hs], preferred_element_type=F32) + bs_ref[sel][:, h:h + 1]
            o_ref[rs, hs] = (z_ref[rs, hs] * s).astype(BF16)

        if c == GATE_CHUNKS - 1:
            @pl.when(i == pl.num_programs(0) - 1)
            def _():
                v_ref[...] = vn


def _gate(z, lg, lb, ws_all, bs_all, n_prompt_chunks):
    rows = z.shape[0]
    tile = GATE_CHUNKS * CHUNK
    assert rows % tile == 0 and rows // CHUNK == n_prompt_chunks + 1
    return pl.pallas_call(
        functools.partial(_gate_kernel, n_prompt_chunks=n_prompt_chunks),
        out_shape=(jax.ShapeDtypeStruct((rows, D_C), BF16),
                   jax.ShapeDtypeStruct((CHUNK, D_C), F32)),
        grid=(rows // tile,),
        in_specs=[
            pl.BlockSpec((tile, 2 * D_C), lambda i: (i, 0)),
            pl.BlockSpec((1, D_C), lambda i: (0, 0)),
            pl.BlockSpec((1, D_C), lambda i: (0, 0)),
            pl.BlockSpec((2, H_C, CHUNK, CHUNK), lambda i: (0, 0, 0, 0)),
            pl.BlockSpec((2, CHUNK, H_C), lambda i: (0, 0, 0)),
        ],
        out_specs=(pl.BlockSpec((tile, D_C), lambda i: (i, 0)),
                   pl.BlockSpec((CHUNK, D_C), lambda i: (0, 0))),
        compiler_params=_params(("arbitrary",)),
        name="gate",
    )(z, lg, lb, ws_all, bs_all)


def kernel(x_prompt, x_sample, state_conv_a, state_conv_b, norm_g, ffn_w1, ffn_w3, ffn_w2,
           ab_w_in, a_conv_w, a_conv_b, a_ln_g, a_ln_b, b_conv_w, ab_w_out,
           c_w_in, c_b_in, c_ln_g, c_ln_b, c_w_s, c_b_s, c_w_out, final_g):
    batch, seq, _ = x_prompt.shape
    nb, dec_seq, _ = x_sample.shape
    depth = norm_g.shape[0]
    n_prompt = batch * seq
    rows = n_prompt + nb
    assert dec_seq == 1 and nb == CHUNK and n_prompt % CHUNK == 0

    row = lambda v: v.reshape(1, -1)
    ffn_stacked = (ffn_w1, ffn_w3, ffn_w2)
    w_bf16 = None

    xs =[x_prompt.reshape(n_prompt, D_MODEL), x_sample.reshape(nb, D_MODEL)]
    state_a_taps = jnp.swapaxes(state_conv_a, 1, 2)
    new_a_p, new_b_p, new_a_s, new_b_s, new_v_s = [], [], [], [], []
    for i in range(depth):
        j = i // 2
        head = None
        if w_bf16 is None:
            (x_head,), xn_head, w_bf16, _ = _ffn(
                xs, row(norm_g[i, 0]), None, row(norm_g[i, 1]), (TM,), tf=TF_F32, n_tiles=1,
                own_f32=(ffn_stacked, (i, 0)), emit_norm=True)
            head = (x_head, xn_head)
        (x,), xn, _, w_bf16 = _ffn(xs, row(norm_g[i, 0]), w_bf16, row(norm_g[i, 1]), (rows,),
                                   tile0=0 if head is None else 1, head=head,
                                   next_w=(ffn_stacked, (i, 1)), emit_norm=True)
        if i % 2 == 0:
            ac, hb, sa_p, sb_p, sa_s, sb_s = _mixer_in(
                xn, ab_w_in[j], a_conv_w[j], row(a_conv_b[j]), b_conv_w[j],
                state_a_taps, state_conv_b, j, batch, seq)
            new_a_p.append(sa_p)
            new_b_p.append(sb_p)
            new_a_s.append(jnp.swapaxes(sa_s, 1, 2))
            new_b_s.append(sb_s)
            x = _mixer_out(ac, hb, row(a_ln_g[j]), row(a_ln_b[j]), ab_w_out[j], x)
        else:
            z = _proj(xn, c_w_in[j], row(c_b_in[j]), True)
            w_s = c_w_s[j][:, :CHUNK, :CHUNK]
            ws_all = jnp.stack([w_s, jnp.broadcast_to(w_s[:, :1, :1], w_s.shape)])
            b_s = c_b_s[j][:, :CHUNK].T
            bs_all = jnp.stack([b_s, jnp.broadcast_to(b_s[:1], b_s.shape)])
            h, v_s = _gate(z, row(c_ln_g[j]), row(c_ln_b[j]), ws_all, bs_all, n_prompt // CHUNK)
            new_v_s.append(v_s.reshape(nb, 1, D_C))
            x = _proj_residual(h, c_w_out[j], x)
        last = i == depth - 1
        xs, _, _, w_bf16 = _ffn([x], row(norm_g[i, 2]), w_bf16, row(final_g),
                                (n_prompt, nb) if last else (rows,),
                                next_w=None if last else (ffn_stacked, (i + 1, 0)),
                                final_norm=last)

    y_prompt, y_sample = xs
    return (y_prompt.reshape(batch, seq, D_MODEL), y_sample.reshape(nb, 1, D_MODEL),
            jnp.concatenate(new_a_p), jnp.concatenate(new_b_p),
            jnp.concatenate(new_a_s), jnp.concatenate(new_b_s),
            jnp.stack(new_v_s))
```

```python
import functools

import jax
import jax.numpy as jnp
from jax import lax
from jax.experimental import pallas as pl
from jax.experimental.pallas import tpu as pltpu

F32 = jnp.float32
BF16 = jnp.bfloat16

D_MODEL = 2048
D_FF = 5632
D_A = 1024
D_B = 1024
K_A = 31
K_B = 3
D_C = 2048
H_C = 8
DH_C = D_C // H_C
CHUNK = 128
EPS = 1e-6

VMEM_LIMIT_BYTES = 56 * 1024 * 1024
LANES = 128
SUBLANES = 8

TM = 1040
NORM_ROWS = 208
TF = 512
TF_F32 = 256
DOWN_CHUNK = 512
TN = 1024
W_STAGE = 256
MIX_T = 1024
MIX_C = 256
SAMPLE_BS = 128
CONV_RB = 64
A_HALO = 32
B_HALO = 8
GATE_CHUNKS = 5


def _params(sem):
    return pltpu.CompilerParams(dimension_semantics=sem, vmem_limit_bytes=VMEM_LIMIT_BYTES)


def _rms_rows(x, g):
    ms = jnp.mean(x * x, axis=-1, keepdims=True)
    return x * lax.rsqrt(ms + EPS) * g


def _layernorm_rows(x, g, b):
    mu = jnp.mean(x, axis=-1, keepdims=True)
    xc = x - mu
    var = jnp.mean(xc * xc, axis=-1, keepdims=True)
    return xc * lax.rsqrt(var + EPS) * g + b


def _norm_chunks(src_ref, g_ref, dst_ref, dtype):
    def body(c, carry):
        r = pl.ds(pl.multiple_of(c * NORM_ROWS, NORM_ROWS), NORM_ROWS)
        dst_ref[r, :] = _rms_rows(src_ref[r, :], g_ref[...]).astype(dtype)
        return carry
    lax.fori_loop(0, TM // NORM_ROWS, body, 0)


def _tile_pieces(tile, seg_rows):
    lo, hi = tile * TM, (tile + 1) * TM
    pieces, base = [], 0
    for k, n in enumerate(seg_rows):
        a, b = max(lo, base), min(hi, base + n)
        if a < b:
            pieces.append((k, a - base, a - lo, b - a))
        base += n
    return pieces


def _ffn_kernel(*refs, in_rows, out_rows, tile0, n_tiles, own_f32, convert_next, emit_norm,
                final_norm, head_rows):
    n_in, n_out = len(in_rows), len(out_rows)
    it = iter(refs)
    x_srcs = [next(it) for _ in range(n_in)]
    g_ref, w1_ref, w3_ref, w2_ref, post_g_ref = (next(it) for _ in range(5))
    next_w = [next(it) for _ in range(3)] if convert_next else []
    head_srcs = [next(it) for _ in range(2)] if head_rows else []
    o_dsts = [next(it) for _ in range(n_out)]
    xn_out = next(it) if emit_norm else None
    own_w_out = [next(it) for _ in range(3)] if own_f32 else []
    next_w_out = [next(it) for _ in range(3)] if convert_next else []
    acc_ref, xn_ref, load_sem, store_sem = (next(it) for _ in range(4))
    xn_next_ref, xn_sem = (next(it), next(it)) if emit_norm else (None, None)
    head_sem = next(it) if head_rows else None

    i = pl.program_id(0)
    j = pl.program_id(1)
    n_i = pl.num_programs(0)
    n_j = pl.num_programs(1)
    tile = i + tile0

    def load_copies(t):
        return [pltpu.make_async_copy(x_srcs[k].at[pl.ds(r0, n)],
                                      acc_ref.at[t % 2, pl.ds(d0, n)], load_sem.at[t % 2])
                for k, r0, d0, n in _tile_pieces(t, in_rows)]

    def store_copies(t):
        return [pltpu.make_async_copy(acc_ref.at[t % 2, pl.ds(d0, n)],
                                      o_dsts[k].at[pl.ds(r0, n)], store_sem.at[t % 2])
                for k, r0, d0, n in _tile_pieces(t, out_rows)]

    def xn_copy(t):
        return pltpu.make_async_copy(xn_next_ref, xn_out.at[pl.ds(t * TM, TM)], xn_sem)

    def head_copies():
        return [pltpu.make_async_copy(head_srcs[0], o_dsts[0].at[pl.ds(0, head_rows)], head_sem.at[0]),
                pltpu.make_async_copy(head_srcs[1], xn_out.at[pl.ds(0, head_rows)], head_sem.at[1])]

    def on_tile(which, fn):
        for t in range(tile0, tile0 + n_tiles):
            @pl.when(which == t)
            def _(t=t):
                fn(t)

    acc = acc_ref.at[lax.rem(tile, 2)]

    def matmuls():
        if own_f32:
            w1, w3, w2 = (w[...].astype(BF16) for w in (w1_ref, w3_ref, w2_ref))
            for dst, w in zip(own_w_out, (w1, w3, w2)):
                dst[...] = w
            w2_cols = lambda cs: w2[:, cs]
        else:
            w1, w3 = w1_ref[...], w3_ref[...]
            w2_cols = lambda cs: w2_ref[:, cs]
        xn = xn_ref[...]
        h1 = jnp.dot(xn, w1, preferred_element_type=F32)
        h3 = jnp.dot(xn, w3, preferred_element_type=F32)
        h = (0.5 * (h1 * jax.nn.sigmoid(h1)) * h3).astype(BF16)
        for n in range(D_MODEL // DOWN_CHUNK):
            cs = slice(n * DOWN_CHUNK, (n + 1) * DOWN_CHUNK)
            acc[:, cs] += jnp.dot(h, w2_cols(cs), preferred_element_type=F32)

    @pl.when(j == 0)
    def _():
        @pl.when(i == 0)
        def _():
            for cp in load_copies(tile0):
                cp.start()
            if head_rows:
                for cp in head_copies():
                    cp.start()
        on_tile(tile, lambda t: [cp.wait() for cp in load_copies(t)])
        for r0 in range(0, TM, NORM_ROWS):
            xn_ref[r0:r0 + NORM_ROWS, :] = _rms_rows(acc[r0:r0 + NORM_ROWS, :],
                                                     g_ref[...]).astype(BF16)
        matmuls()

    @pl.when(j == 2)
    def _():
        @pl.when(i > 0)
        def _():
            on_tile(tile - 1, lambda t: [cp.wait() for cp in store_copies(t)])
            if emit_norm:
                on_tile(tile - 1, lambda t: xn_copy(t).wait())

        @pl.when(i + 1 < n_i)
        def _():
            on_tile(tile + 1, lambda t: [cp.start() for cp in load_copies(t)])

    @pl.when(j > 0)
    def _():
        matmuls()

    if convert_next:
        for src, dst in zip(next_w, next_w_out):
            dst[...] = src[...].astype(BF16)

    @pl.when(j == n_j - 1)
    def _():
        if final_norm:
            _norm_chunks(acc, post_g_ref, acc, F32)
        on_tile(tile, lambda t: [cp.start() for cp in store_copies(t)])
        if emit_norm:
            _norm_chunks(acc, post_g_ref, xn_next_ref, BF16)
            on_tile(tile, lambda t: xn_copy(t).start())

        @pl.when(i == n_i - 1)
        def _():
            on_tile(tile, lambda t: [cp.wait() for cp in store_copies(t)])
            if emit_norm:
                on_tile(tile, lambda t: xn_copy(t).wait())
            if head_rows:
                for cp in head_copies():
                    cp.wait()


def _round_up(n, m):
    return -(-n // m) * m


def _ffn(xs, g, w_bf16, post_g, out_rows, *, tf=TF, tile0=0, n_tiles=None, own_f32=None,
         next_w=None, emit_norm=False, final_norm=False, head=None):
    in_rows = tuple(a.shape[0] for a in xs)
    rows = sum(in_rows)
    assert rows % TM == 0 and D_FF % tf == 0 and D_MODEL % DOWN_CHUNK == 0
    n_i = rows // TM - tile0 if n_tiles is None else n_tiles
    n_j = D_FF // tf
    out_total = sum(out_rows)
    assert n_j > 2 and out_total == (tile0 + n_i) * TM
    head_rows = 0 if head is None else head[0].shape[0]
    assert head_rows == (tile0 * TM if head is not None else 0) and (emit_norm or head is None)
    convert_next = next_w is not None
    any_spec = pl.BlockSpec(memory_space=pl.ANY)
    row_spec = pl.BlockSpec((1, D_MODEL), lambda i, j: (0, 0))

    if own_f32 is None:
        w_specs = [pl.BlockSpec((D_MODEL, tf), lambda i, j: (0, j)),
                   pl.BlockSpec((D_MODEL, tf), lambda i, j: (0, j)),
                   pl.BlockSpec((tf, D_MODEL), lambda i, j: (j, 0))]
        w_args = list(w_bf16)
    else:
        own_stacked, (oli, osi) = own_f32
        w_specs = [pl.BlockSpec((None, None, D_MODEL, tf), lambda i, j: (oli, osi, 0, j)),
                   pl.BlockSpec((None, None, D_MODEL, tf), lambda i, j: (oli, osi, 0, j)),
                   pl.BlockSpec((None, None, tf, D_MODEL), lambda i, j: (oli, osi, j, 0))]
        w_args = list(own_stacked)

    in_specs = [any_spec] * len(xs) + [row_spec] + w_specs + [row_spec]
    args = list(xs) + [g] + w_args + [post_g]
    out_shape = [jax.ShapeDtypeStruct((n, D_MODEL), F32) for n in out_rows]
    out_specs = [any_spec] * len(out_rows)
    if convert_next:
        stacked, (li, si) = next_w
        pr = _round_up(pl.cdiv(D_MODEL, n_i), 2 * SUBLANES)
        pd = _round_up(pl.cdiv(D_FF, n_i * n_j), 2 * SUBLANES)
        last_pd = pl.cdiv(D_FF, pd) - 1
        down_idx = lambda i, j: jnp.minimum(i * n_j + j, last_pd)
        assert (n_i - 1) * pr < D_MODEL
        in_specs += [
            pl.BlockSpec((None, None, pr, tf), lambda i, j: (li, si, i, j)),
            pl.BlockSpec((None, None, pr, tf), lambda i, j: (li, si, i, j)),
            pl.BlockSpec((None, None, pd, D_MODEL), lambda i, j: (li, si, down_idx(i, j), 0)),
        ]
        args += list(stacked)
    if head is not None:
        in_specs += [any_spec, any_spec]
        args += list(head)
    if emit_norm:
        out_shape.append(jax.ShapeDtypeStruct((out_total, D_MODEL), BF16))
        out_specs.append(any_spec)
    if own_f32 is not None:
        out_shape += [jax.ShapeDtypeStruct(w.shape[2:], BF16) for w in own_stacked]
        out_specs += [pl.BlockSpec((D_MODEL, tf), lambda i, j: (0, j)),
                      pl.BlockSpec((D_MODEL, tf), lambda i, j: (0, j)),
                      pl.BlockSpec((tf, D_MODEL), lambda i, j: (j, 0))]
    if convert_next:
        out_shape += [jax.ShapeDtypeStruct(w.shape[2:], BF16) for w in stacked]
        out_specs += [pl.BlockSpec((pr, tf), lambda i, j: (i, j)),
                      pl.BlockSpec((pr, tf), lambda i, j: (i, j)),
                      pl.BlockSpec((pd, D_MODEL), lambda i, j: (down_idx(i, j), 0))]
    scratch = [pltpu.VMEM((2, TM, D_MODEL), F32), pltpu.VMEM((TM, D_MODEL), BF16),
               pltpu.SemaphoreType.DMA((2,)), pltpu.SemaphoreType.DMA((2,))]
    if emit_norm:
        scratch += [pltpu.VMEM((TM, D_MODEL), BF16), pltpu.SemaphoreType.DMA(())]
    if head is not None:
        scratch += [pltpu.SemaphoreType.DMA((2,))]

    res = pl.pallas_call(
        functools.partial(_ffn_kernel, in_rows=in_rows, out_rows=tuple(out_rows), tile0=tile0,
                          n_tiles=n_i, own_f32=own_f32 is not None, convert_next=convert_next,
                          emit_norm=emit_norm, final_norm=final_norm, head_rows=head_rows),
        out_shape=out_shape,
        grid=(n_i, n_j),
        in_specs=in_specs,
        out_specs=out_specs,
        scratch_shapes=scratch,
        compiler_params=_params(("arbitrary", "arbitrary")),
        name="ffn",
    )(*args)
    res = list(res)
    outs = [res.pop(0) for _ in out_rows]
    xn = res.pop(0) if emit_norm else None
    own = [res.pop(0) for _ in range(3)] if own_f32 is not None else None
    return outs, xn, own, (res if convert_next else None)


def _proj_residual_kernel(h_ref, w_ref, x_ref, o_ref, wb_ref):
    @pl.when(pl.program_id(1) == 0)
    def _():
        wb_ref[...] = w_ref[...].astype(BF16)

    o_ref[...] = x_ref[...] + jnp.dot(h_ref[...], wb_ref[...], preferred_element_type=F32)


def _proj_residual(h, w, x):
    rows, k = h.shape
    return pl.pallas_call(
        _proj_residual_kernel,
        out_shape=jax.ShapeDtypeStruct((rows, D_MODEL), F32),
        grid=(D_MODEL // TN, rows // TM),
        in_specs=[
            pl.BlockSpec((TM, k), lambda j, i: (i, 0)),
            pl.BlockSpec((k, TN), lambda j, i: (0, j)),
            pl.BlockSpec((TM, TN), lambda j, i: (i, j)),
        ],
        out_specs=pl.BlockSpec((TM, TN), lambda j, i: (i, j)),
        scratch_shapes=[pltpu.VMEM((k, TN), BF16)],
        compiler_params=_params(("arbitrary", "arbitrary")),
        name="proj_residual",
    )(h, w, x)


def _conv_a(abuf, sbuf, wa_ref, n_rows):
    shift_rows = A_HALO + n_rows - SUBLANES
    for s in range(1, SUBLANES):
        sbuf[s - 1, 0:shift_rows, :] = abuf[s:s + shift_rows, :]

    a0 = A_HALO - (K_A - 1)

    def tap(k, r0, cs):
        q, s = divmod(a0 + k, SUBLANES)
        src = abuf if s == 0 else sbuf.at[s - 1]
        return wa_ref[k:k + 1, cs] * src[SUBLANES * q + r0:SUBLANES * q + r0 + CONV_RB, cs]

    for c in range(MIX_C // LANES):
        cs = slice(c * LANES, (c + 1) * LANES)
        for r0 in range(0, n_rows, CONV_RB):
            acc = tap(0, r0, cs)
            for k in range(1, K_A):
                acc = acc + tap(k, r0, cs)
            yield slice(r0, r0 + CONV_RB), cs, acc


def _load_bf16_columns(w_hbm, col_starts, width, dsts, stage, sem):
    def chunk(c):
        return pltpu.make_async_copy(w_hbm.at[:, pl.ds(col_starts[c], width)], stage.at[c % 2],
                                     sem.at[c % 2])
    chunk(0).start()
    for c in range(len(col_starts)):
        if c + 1 < len(col_starts):
            chunk(c + 1).start()
        chunk(c).wait()
        dsts[c][...] = stage[c % 2].astype(BF16)


def _mixer_in_kernel(xn_ref, w_hbm, wa_ref, ba_ref, wb_ref, sa_ref, sb_ref,
                     ac_ref, hb_ref, nap_ref, nbp_ref, nas_ref, nbs_ref,
                     wbf, abuf, sbuf, bbuf, stage, sem, *, n_prompt_tiles, tiles_per_seq):
    c_tile = pl.program_id(0)
    r = pl.program_id(1)
    n_groups = 5
    wcur = wbf.at[lax.rem(c_tile, 2)]
    wnext = wbf.at[1 - lax.rem(c_tile, 2)]

    @pl.when((c_tile == 0) & (r == 0))
    def _():
        _load_bf16_columns(w_hbm, [g * D_A for g in range(n_groups)], MIX_C,
                           [wcur.at[:, pl.ds(g * MIX_C, MIX_C)] for g in range(n_groups)],
                           stage, sem)

    @pl.when(c_tile + 1 < pl.num_programs(0))
    def _():
        col = pl.multiple_of((c_tile + 1) * MIX_C, MIX_C)

        def fetch(g):
            return pltpu.make_async_copy(w_hbm.at[:, pl.ds(g * D_A + col, MIX_C)],
                                         stage.at[g % 2], sem.at[g % 2])
        for g in range(n_groups):
            @pl.when(r == g + 2)
            def _(g=g):
                fetch(g).wait()
                wnext[:, g * MIX_C:(g + 1) * MIX_C] = stage[g % 2].astype(BF16)

            @pl.when(r == g + 1)
            def _(g=g):
                fetch(g).start()

    def project(xn):
        p = jnp.dot(xn, wcur[...], preferred_element_type=F32)
        return [p[:, g * MIX_C:(g + 1) * MIX_C] for g in range(n_groups)]

    @pl.when(r < n_prompt_tiles)
    def _():
        @pl.when(lax.rem(r, tiles_per_seq) == 0)
        def _():
            abuf[0:A_HALO, :] = jnp.zeros((A_HALO, MIX_C), F32)
            bbuf[0:B_HALO, :] = jnp.zeros((B_HALO, MIX_C), F32)

        pa, ga, bg, cg, hb = project(xn_ref[...])
        abuf[A_HALO:A_HALO + MIX_T, :] = pa * jax.nn.sigmoid(ga)
        bbuf[B_HALO:B_HALO + MIX_T, :] = cg * hb

        for rs, cs, acc in _conv_a(abuf, sbuf, wa_ref, MIX_T):
            ac_ref[rs, cs] = acc + ba_ref[:, cs]

        b0 = B_HALO - (K_B - 1)
        b_c = wb_ref[0:1, :] * bbuf[b0:b0 + MIX_T, :]
        for k in range(1, K_B):
            b_c = b_c + wb_ref[k:k + 1, :] * bbuf[b0 + k:b0 + k + MIX_T, :]
        hb_ref[...] = (bg * b_c).astype(BF16)

        @pl.when(lax.rem(r, tiles_per_seq) == tiles_per_seq - 1)
        def _():
            nap_ref[0] = abuf[MIX_T + A_HALO - (K_A - 1):MIX_T + A_HALO, :]
            nbp_ref[0] = bbuf[MIX_T + b0:MIX_T + B_HALO, :]

        abuf[0:A_HALO, :] = abuf[MIX_T:MIX_T + A_HALO, :]
        bbuf[0:B_HALO, :] = bbuf[MIX_T:MIX_T + B_HALO, :]

    @pl.when(r >= n_prompt_tiles)
    def _():
        rs = pl.ds(pl.multiple_of((r - n_prompt_tiles) * SAMPLE_BS, SAMPLE_BS), SAMPLE_BS)
        pa, ga, bg, cg, hb = project(xn_ref[rs, :])
        a = pa * jax.nn.sigmoid(ga)
        acc = wa_ref[K_A - 1:K_A, :] * a
        for k in range(K_A - 1):
            acc = acc + wa_ref[k:k + 1, :] * sa_ref[k]
        ac_ref[rs, :] = acc + ba_ref[...]
        for k in range(K_A - 2):
            nas_ref[k] = sa_ref[k + 1]
        nas_ref[K_A - 2] = a

        ch = cg * hb
        b_c = wb_ref[K_B - 1:K_B, :] * ch
        for k in range(K_B - 1):
            b_c = b_c + wb_ref[k:k + 1, :] * sb_ref[:, k, :]
        hb_ref[rs, :] = (bg * b_c).astype(BF16)
        for k in range(K_B - 2):
            nbs_ref[:, k, :] = sb_ref[:, k + 1, :]
        nbs_ref[:, K_B - 2, :] = ch


def _mixer_in(xn, w_in, wa, ba, wb, sa, sb, layer, batch, seq):
    rows = xn.shape[0]
    n_sample = sa.shape[2]
    n_prompt_tiles = batch * seq // MIX_T
    tiles_per_seq = seq // MIX_T
    assert rows == batch * seq + n_sample and n_sample <= MIX_T and seq % MIX_T == 0
    assert n_sample % SAMPLE_BS == 0
    assert n_prompt_tiles + n_sample // SAMPLE_BS >= 7
    row_blk = lambda r: jnp.minimum(r, n_prompt_tiles)
    smp_blk = lambda r: jnp.maximum(r - n_prompt_tiles, 0)
    seq_of = lambda r: jnp.minimum(r, n_prompt_tiles - 1) // tiles_per_seq
    chan = lambda k: pl.BlockSpec((k, MIX_C), lambda c, r: (0, c))
    state = lambda k, lyr: pl.BlockSpec((None, SAMPLE_BS, k, MIX_C),
                                        lambda c, r: (lyr, smp_blk(r), 0, c))
    tap_major = lambda k, lyr: pl.BlockSpec((None, k, SAMPLE_BS, MIX_C),
                                            lambda c, r: (lyr, 0, smp_blk(r), c))
    prompt_state = lambda k: pl.BlockSpec((None, 1, k, MIX_C), lambda c, r: (0, seq_of(r), 0, c))
    return pl.pallas_call(
        functools.partial(_mixer_in_kernel, n_prompt_tiles=n_prompt_tiles,
                          tiles_per_seq=tiles_per_seq),
        out_shape=(jax.ShapeDtypeStruct((rows, D_A), F32),
                   jax.ShapeDtypeStruct((rows, D_B), BF16),
                   jax.ShapeDtypeStruct((1, batch, K_A - 1, D_A), F32),
                   jax.ShapeDtypeStruct((1, batch, K_B - 1, D_B), F32),
                   jax.ShapeDtypeStruct((1, K_A - 1, n_sample, D_A), F32),
                   jax.ShapeDtypeStruct((1, n_sample, K_B - 1, D_B), F32)),
        grid=(D_A // MIX_C, n_prompt_tiles + n_sample // SAMPLE_BS),
        in_specs=[pl.BlockSpec((MIX_T, D_MODEL), lambda c, r: (row_blk(r), 0)),
                  pl.BlockSpec(memory_space=pl.ANY),
                  chan(K_A), chan(1), chan(K_B),
                  tap_major(K_A - 1, layer), state(K_B - 1, layer)],
        out_specs=(pl.BlockSpec((MIX_T, MIX_C), lambda c, r: (row_blk(r), c)),
                   pl.BlockSpec((MIX_T, MIX_C), lambda c, r: (row_blk(r), c)),
                   prompt_state(K_A - 1), prompt_state(K_B - 1),
                   tap_major(K_A - 1, 0), state(K_B - 1, 0)),
        scratch_shapes=[pltpu.VMEM((2, D_MODEL, 5 * MIX_C), BF16),
                        pltpu.VMEM((A_HALO + MIX_T, MIX_C), F32),
                        pltpu.VMEM((SUBLANES - 1, A_HALO + MIX_T - SUBLANES, MIX_C), F32),
                        pltpu.VMEM((B_HALO + MIX_T, MIX_C), F32),
                        pltpu.VMEM((2, D_MODEL, MIX_C), F32),
                        pltpu.SemaphoreType.DMA((2,))],
        compiler_params=_params(("arbitrary", "arbitrary")),
        name="mixer_in",
    )(xn, w_in, wa, ba, wb, sa, sb)


def _mixer_out_kernel(ac_ref, hb_ref, lg_ref, lb_ref, w_hbm, x_ref, o_ref, wb_ref, stage, sem):
    j = pl.program_id(0)

    @pl.when(pl.program_id(1) == 0)
    def _():
        col = pl.multiple_of(j * TN, TN)
        offs = range(0, TN, W_STAGE)
        _load_bf16_columns(w_hbm, [col + o for o in offs], W_STAGE,
                           [wb_ref.at[:, pl.ds(o, W_STAGE)] for o in offs], stage, sem)

    a_ln = _layernorm_rows(ac_ref[...], lg_ref[...], lb_ref[...])
    h_a = (a_ln * jax.nn.sigmoid(a_ln)).astype(BF16)
    y = jnp.dot(h_a, wb_ref[0:D_A, :], preferred_element_type=F32)
    y = y + jnp.dot(hb_ref[...], wb_ref[D_A:D_A + D_B, :], preferred_element_type=F32)
    o_ref[...] = x_ref[...] + y


def _mixer_out(ac, hb, lg, lb, w, x):
    rows = x.shape[0]
    return pl.pallas_call(
        _mixer_out_kernel,
        out_shape=jax.ShapeDtypeStruct((rows, D_MODEL), F32),
        grid=(D_MODEL // TN, rows // TM),
        in_specs=[
            pl.BlockSpec((TM, D_A), lambda j, i: (i, 0)),
            pl.BlockSpec((TM, D_B), lambda j, i: (i, 0)),
            pl.BlockSpec((1, D_A), lambda j, i: (0, 0)),
            pl.BlockSpec((1, D_A), lambda j, i: (0, 0)),
            pl.BlockSpec(memory_space=pl.ANY),
            pl.BlockSpec((TM, TN), lambda j, i: (i, j)),
        ],
        out_specs=pl.BlockSpec((TM, TN), lambda j, i: (i, j)),
        scratch_shapes=[pltpu.VMEM((D_A + D_B, TN), BF16),
                        pltpu.VMEM((2, D_A + D_B, W_STAGE), F32),
                        pltpu.SemaphoreType.DMA((2,))],
        compiler_params=_params(("arbitrary", "arbitrary")),
        name="mixer_out",
    )(ac, hb, lg, lb, w, x)


def _gelu(y):
    return 0.5 * y * (1.0 + lax.erf(y * (2.0 ** -0.5)))


def _gmlp_in_kernel(xn_ref, w_hbm, b_ref, lg_ref, lb_ref, ws_ref, bs_ref, o_ref, v_ref,
                    wb_ref, s_ref, stage, sem, *, n_prompt_chunks):
    i = pl.program_id(0)

    @pl.when(i == 0)
    def _():
        offs = list(range(0, 2 * D_C, W_STAGE))
        _load_bf16_columns(w_hbm, offs, W_STAGE,
                           [wb_ref.at[:, pl.ds(o, W_STAGE)] for o in offs], stage, sem)

    xn = xn_ref[...]
    zv = _gelu(jnp.dot(xn, wb_ref[:, D_C:2 * D_C], preferred_element_type=F32)
               + b_ref[:, D_C:2 * D_C])
    row = lax.broadcasted_iota(jnp.int32, (CHUNK, CHUNK), 0)
    col = lax.broadcasted_iota(jnp.int32, (CHUNK, CHUNK), 1)
    for c in range(GATE_CHUNKS):
        rs = slice(c * CHUNK, (c + 1) * CHUNK)
        is_sample = i * GATE_CHUNKS + c >= n_prompt_chunks
        sel = jnp.where(is_sample, 1, 0)
        vn = _layernorm_rows(zv[rs, :], lg_ref[...], lb_ref[...])
        vb = vn.astype(BF16)
        keep = (col <= row) & (col >= jnp.where(is_sample, row, 0))
        for h in range(H_C):
            hs = slice(h * DH_C, (h + 1) * DH_C)
            w = jnp.where(keep, ws_ref[sel, h], 0.0).astype(BF16)
            s_ref[rs, hs] = (jnp.dot(w, vb[:, hs], preferred_element_type=F32)
                             + bs_ref[sel][:, h:h + 1])

        if c == GATE_CHUNKS - 1:
            @pl.when(i == pl.num_programs(0) - 1)
            def _():
                v_ref[...] = vn

    zu = _gelu(jnp.dot(xn, wb_ref[:, 0:D_C], preferred_element_type=F32) + b_ref[:, 0:D_C])
    o_ref[...] = (zu * s_ref[...]).astype(BF16)


def _gmlp_in(xn, w_in, b_in, lg, lb, ws_all, bs_all, n_prompt_chunks):
    rows = xn.shape[0]
    tile = GATE_CHUNKS * CHUNK
    assert rows % tile == 0 and rows // CHUNK == n_prompt_chunks + 1
    return pl.pallas_call(
        functools.partial(_gmlp_in_kernel, n_prompt_chunks=n_prompt_chunks),
        out_shape=(jax.ShapeDtypeStruct((rows, D_C), BF16),
                   jax.ShapeDtypeStruct((CHUNK, D_C), F32)),
        grid=(rows // tile,),
        in_specs=[
            pl.BlockSpec((tile, D_MODEL), lambda i: (i, 0)),
            pl.BlockSpec(memory_space=pl.ANY),
            pl.BlockSpec((1, 2 * D_C), lambda i: (0, 0)),
            pl.BlockSpec((1, D_C), lambda i: (0, 0)),
            pl.BlockSpec((1, D_C), lambda i: (0, 0)),
            pl.BlockSpec((2, H_C, CHUNK, CHUNK), lambda i: (0, 0, 0, 0)),
            pl.BlockSpec((2, CHUNK, H_C), lambda i: (0, 0, 0)),
        ],
        out_specs=(pl.BlockSpec((tile, D_C), lambda i: (i, 0)),
                   pl.BlockSpec((CHUNK, D_C), lambda i: (0, 0))),
        scratch_shapes=[pltpu.VMEM((D_MODEL, 2 * D_C), BF16),
                        pltpu.VMEM((tile, D_C), F32),
                        pltpu.VMEM((2, D_MODEL, W_STAGE), F32),
                        pltpu.SemaphoreType.DMA((2,))],
        compiler_params=_params(("arbitrary",)),
        name="gmlp_in",
    )(xn, w_in, b_in, lg, lb, ws_all, bs_all)


def kernel(x_prompt, x_sample, state_conv_a, state_conv_b, norm_g, ffn_w1, ffn_w3, ffn_w2,
           ab_w_in, a_conv_w, a_conv_b, a_ln_g, a_ln_b, b_conv_w, ab_w_out,
           c_w_in, c_b_in, c_ln_g, c_ln_b, c_w_s, c_b_s, c_w_out, final_g):
    batch, seq, _ = x_prompt.shape
    nb, dec_seq, _ = x_sample.shape
    depth = norm_g.shape[0]
    n_prompt = batch * seq
    rows = n_prompt + nb
    assert dec_seq == 1 and nb == CHUNK and n_prompt % CHUNK == 0

    row = lambda v: v.reshape(1, -1)
    ffn_stacked = (ffn_w1, ffn_w3, ffn_w2)
    w_bf16 = None

    xs =[x_prompt.reshape(n_prompt, D_MODEL), x_sample.reshape(nb, D_MODEL)]
    state_a_taps = jnp.swapaxes(state_conv_a, 1, 2)
    new_a_p, new_b_p, new_a_s, new_b_s, new_v_s = [], [], [], [], []
    for i in range(depth):
        j = i // 2
        head = None
        if w_bf16 is None:
            (x_head,), xn_head, w_bf16, _ = _ffn(
                xs, row(norm_g[i, 0]), None, row(norm_g[i, 1]), (TM,), tf=TF_F32, n_tiles=1,
                own_f32=(ffn_stacked, (i, 0)), emit_norm=True)
            head = (x_head, xn_head)
        (x,), xn, _, w_bf16 = _ffn(xs, row(norm_g[i, 0]), w_bf16, row(norm_g[i, 1]), (rows,),
                                   tile0=0 if head is None else 1, head=head,
                                   next_w=(ffn_stacked, (i, 1)), emit_norm=True)
        if i % 2 == 0:
            ac, hb, sa_p, sb_p, sa_s, sb_s = _mixer_in(
                xn, ab_w_in[j], a_conv_w[j], row(a_conv_b[j]), b_conv_w[j],
                state_a_taps, state_conv_b, j, batch, seq)
            new_a_p.append(sa_p)
            new_b_p.append(sb_p)
            new_a_s.append(jnp.swapaxes(sa_s, 1, 2))
            new_b_s.append(sb_s)
            x = _mixer_out(ac, hb, row(a_ln_g[j]), row(a_ln_b[j]), ab_w_out[j], x)
        else:
            w_s =c_w_s[j][:, :CHUNK, :CHUNK]
            ws_all = jnp.stack([w_s, jnp.broadcast_to(w_s[:, :1, :1], w_s.shape)])
            b_s = c_b_s[j][:, :CHUNK].T
            bs_all = jnp.stack([b_s, jnp.broadcast_to(b_s[:1], b_s.shape)])
            h, v_s = _gmlp_in(xn, c_w_in[j], row(c_b_in[j]), row(c_ln_g[j]), row(c_ln_b[j]),
                              ws_all, bs_all, n_prompt // CHUNK)
            new_v_s.append(v_s.reshape(nb, 1, D_C))
            x = _proj_residual(h, c_w_out[j], x)
        last = i == depth - 1
        xs, _, _, w_bf16 = _ffn([x], row(norm_g[i, 2]), w_bf16, row(final_g),
                                (n_prompt, nb) if last else (rows,),
                                next_w=None if last else (ffn_stacked, (i + 1, 0)),
                                final_norm=last)

    y_prompt, y_sample = xs
    return (y_prompt.reshape(batch, seq, D_MODEL), y_sample.reshape(nb, 1, D_MODEL),
            jnp.concatenate(new_a_p), jnp.concatenate(new_b_p),
            jnp.concatenate(new_a_s), jnp.concatenate(new_b_s),
            jnp.stack(new_v_s))
```

```python
import functools

import jax
import jax.numpy as jnp
from jax import lax
from jax.experimental import pallas as pl
from jax.experimental.pallas import tpu as pltpu

F32 = jnp.float32
BF16 = jnp.bfloat16

D_MODEL = 2048
D_FF = 5632
D_A = 1024
D_B = 1024
K_A = 31
K_B = 3
D_C = 2048
H_C = 8
DH_C = D_C // H_C
CHUNK = 128
EPS = 1e-6

VMEM_LIMIT_BYTES = 56 * 1024 * 1024
LANES = 128
SUBLANES = 8

TM = 1040
NORM_ROWS = 208
TF = 512
TF_F32 = 256
DOWN_CHUNK = 512
TN = 1024
W_STAGE = 256
MIX_T = 1024
MIX_C = 256
SAMPLE_BS = 128
CONV_RB = 64
A_HALO = 32
B_HALO = 8
GATE_CHUNKS = 5


def _params(sem):
    return pltpu.CompilerParams(dimension_semantics=sem, vmem_limit_bytes=VMEM_LIMIT_BYTES)


def _rms_rows(x, g):
    ms = jnp.mean(x * x, axis=-1, keepdims=True)
    return x * lax.rsqrt(ms + EPS) * g


def _layernorm_rows(x, g, b):
    mu = jnp.mean(x, axis=-1, keepdims=True)
    xc = x - mu
    var = jnp.mean(xc * xc, axis=-1, keepdims=True)
    return xc * lax.rsqrt(var + EPS) * g + b


def _norm_chunks(src_ref, g_ref, dst_ref, dtype):
    def body(c, carry):
        r = pl.ds(pl.multiple_of(c * NORM_ROWS, NORM_ROWS), NORM_ROWS)
        dst_ref[r, :] = _rms_rows(src_ref[r, :], g_ref[...]).astype(dtype)
        return carry
    lax.fori_loop(0, TM // NORM_ROWS, body, 0)


def _tile_pieces(tile, seg_rows):
    lo, hi = tile * TM, (tile + 1) * TM
    pieces, base = [], 0
    for k, n in enumerate(seg_rows):
        a, b = max(lo, base), min(hi, base + n)
        if a < b:
            pieces.append((k, a - base, a - lo, b - a))
        base += n
    return pieces


def _ffn_kernel(*refs, in_rows, out_rows, tile0, n_tiles, own_f32, convert_next, emit_norm,
                final_norm, head_rows):
    n_in, n_out = len(in_rows), len(out_rows)
    it = iter(refs)
    x_srcs = [next(it) for _ in range(n_in)]
    g_ref, w1_ref, w3_ref, w2_ref, post_g_ref = (next(it) for _ in range(5))
    next_w = [next(it) for _ in range(3)] if convert_next else []
    head_srcs = [next(it) for _ in range(2)] if head_rows else []
    o_dsts = [next(it) for _ in range(n_out)]
    xn_out = next(it) if emit_norm else None
    own_w_out = [next(it) for _ in range(3)] if own_f32 else []
    next_w_out = [next(it) for _ in range(3)] if convert_next else []
    acc_ref, xn_ref, load_sem, store_sem = (next(it) for _ in range(4))
    xn_next_ref, xn_sem = (next(it), next(it)) if emit_norm else (None, None)
    head_sem = next(it) if head_rows else None

    i = pl.program_id(0)
    j = pl.program_id(1)
    n_i = pl.num_programs(0)
    n_j = pl.num_programs(1)
    tile = i + tile0

    def load_copies(t):
        return [pltpu.make_async_copy(x_srcs[k].at[pl.ds(r0, n)],
                                      acc_ref.at[t % 2, pl.ds(d0, n)], load_sem.at[t % 2])
                for k, r0, d0, n in _tile_pieces(t, in_rows)]

    def store_copies(t):
        return [pltpu.make_async_copy(acc_ref.at[t % 2, pl.ds(d0, n)],
                                      o_dsts[k].at[pl.ds(r0, n)], store_sem.at[t % 2])
                for k, r0, d0, n in _tile_pieces(t, out_rows)]

    def xn_copy(t):
        return pltpu.make_async_copy(xn_next_ref, xn_out.at[pl.ds(t * TM, TM)], xn_sem)

    def head_copies():
        return [pltpu.make_async_copy(head_srcs[0], o_dsts[0].at[pl.ds(0, head_rows)], head_sem.at[0]),
                pltpu.make_async_copy(head_srcs[1], xn_out.at[pl.ds(0, head_rows)], head_sem.at[1])]

    def on_tile(which, fn):
        for t in range(tile0, tile0 + n_tiles):
            @pl.when(which == t)
            def _(t=t):
                fn(t)

    acc = acc_ref.at[lax.rem(tile, 2)]

    def matmuls():
        if own_f32:
            w1, w3, w2 = (w[...].astype(BF16) for w in (w1_ref, w3_ref, w2_ref))
            for dst, w in zip(own_w_out, (w1, w3, w2)):
                dst[...] = w
            w2_cols = lambda cs: w2[:, cs]
        else:
            w1, w3 = w1_ref[...], w3_ref[...]
            w2_cols = lambda cs: w2_ref[:, cs]
        xn = xn_ref[...]
        h1 = jnp.dot(xn, w1, preferred_element_type=F32)
        h3 = jnp.dot(xn, w3, preferred_element_type=F32)
        h = (0.5 * (h1 * jax.nn.sigmoid(h1)) * h3).astype(BF16)
        for n in range(D_MODEL // DOWN_CHUNK):
            cs = slice(n * DOWN_CHUNK, (n + 1) * DOWN_CHUNK)
            acc[:, cs] += jnp.dot(h, w2_cols(cs), preferred_element_type=F32)
        for src, dst in zip(next_w, next_w_out):
            dst[...] = src[...].astype(BF16)

    @pl.when(j == 0)
    def _():
        @pl.when(i == 0)
        def _():
            for cp in load_copies(tile0):
                cp.start()
            if head_rows:
                for cp in head_copies():
                    cp.start()
        on_tile(tile, lambda t: [cp.wait() for cp in load_copies(t)])
        for r0 in range(0, TM, NORM_ROWS):
            xn_ref[r0:r0 + NORM_ROWS, :] = _rms_rows(acc[r0:r0 + NORM_ROWS, :],
                                                     g_ref[...]).astype(BF16)
        matmuls()

    @pl.when(j == 2)
    def _():
        @pl.when(i > 0)
        def _():
            on_tile(tile - 1, lambda t: [cp.wait() for cp in store_copies(t)])
            if emit_norm:
                on_tile(tile - 1, lambda t: xn_copy(t).wait())

        @pl.when(i + 1 < n_i)
        def _():
            on_tile(tile + 1, lambda t: [cp.start() for cp in load_copies(t)])

    @pl.when(j > 0)
    def _():
        matmuls()

    @pl.when(j == n_j - 1)
    def _():
        if final_norm:
            _norm_chunks(acc, post_g_ref, acc, F32)
        on_tile(tile, lambda t: [cp.start() for cp in store_copies(t)])
        if emit_norm:
            _norm_chunks(acc, post_g_ref, xn_next_ref, BF16)
            on_tile(tile, lambda t: xn_copy(t).start())

        @pl.when(i == n_i - 1)
        def _():
            on_tile(tile, lambda t: [cp.wait() for cp in store_copies(t)])
            if emit_norm:
                on_tile(tile, lambda t: xn_copy(t).wait())
            if head_rows:
                for cp in head_copies():
                    cp.wait()


def _round_up(n, m):
    return -(-n // m) * m


def _ffn(xs, g, w_bf16, post_g, out_rows, *, tf=TF, tile0=0, n_tiles=None, own_f32=None,
         next_w=None, emit_norm=False, final_norm=False, head=None):
    in_rows = tuple(a.shape[0] for a in xs)
    rows = sum(in_rows)
    assert rows % TM == 0 and D_FF % tf == 0 and D_MODEL % DOWN_CHUNK == 0
    n_i = rows // TM - tile0 if n_tiles is None else n_tiles
    n_j = D_FF // tf
    out_total = sum(out_rows)
    assert n_j > 2 and out_total == (tile0 + n_i) * TM
    head_rows = 0 if head is None else head[0].shape[0]
    assert head_rows == (tile0 * TM if head is not None else 0) and (emit_norm or head is None)
    convert_next = next_w is not None
    any_spec = pl.BlockSpec(memory_space=pl.ANY)
    row_spec = pl.BlockSpec((1, D_MODEL), lambda i, j: (0, 0))

    if own_f32 is None:
        w_specs = [pl.BlockSpec((D_MODEL, tf), lambda i, j: (0, j)),
                   pl.BlockSpec((D_MODEL, tf), lambda i, j: (0, j)),
                   pl.BlockSpec((tf, D_MODEL), lambda i, j: (j, 0))]
        w_args = list(w_bf16)
    else:
        own_stacked, (oli, osi) = own_f32
        w_specs = [pl.BlockSpec((None, None, D_MODEL, tf), lambda i, j: (oli, osi, 0, j)),
                   pl.BlockSpec((None, None, D_MODEL, tf), lambda i, j: (oli, osi, 0, j)),
                   pl.BlockSpec((None, None, tf, D_MODEL), lambda i, j: (oli, osi, j, 0))]
        w_args = list(own_stacked)

    in_specs = [any_spec] * len(xs) + [row_spec] + w_specs + [row_spec]
    args = list(xs) + [g] + w_args + [post_g]
    out_shape = [jax.ShapeDtypeStruct((n, D_MODEL), F32) for n in out_rows]
    out_specs = [any_spec] * len(out_rows)
    if convert_next:
        stacked, (li, si) = next_w
        pr = _round_up(pl.cdiv(D_MODEL, n_i), 2 * SUBLANES)
        pd = _round_up(pl.cdiv(D_FF, n_i * n_j), 2 * SUBLANES)
        last_pd = pl.cdiv(D_FF, pd) - 1
        down_idx = lambda i, j: jnp.minimum(i * n_j + j, last_pd)
        assert (n_i - 1) * pr < D_MODEL
        in_specs += [
            pl.BlockSpec((None, None, pr, tf), lambda i, j: (li, si, i, j)),
            pl.BlockSpec((None, None, pr, tf), lambda i, j: (li, si, i, j)),
            pl.BlockSpec((None, None, pd, D_MODEL), lambda i, j: (li, si, down_idx(i, j), 0)),
        ]
        args += list(stacked)
    if head is not None:
        in_specs += [any_spec, any_spec]
        args += list(head)
    if emit_norm:
        out_shape.append(jax.ShapeDtypeStruct((out_total, D_MODEL), BF16))
        out_specs.append(any_spec)
    if own_f32 is not None:
        out_shape += [jax.ShapeDtypeStruct(w.shape[2:], BF16) for w in own_stacked]
        out_specs += [pl.BlockSpec((D_MODEL, tf), lambda i, j: (0, j)),
                      pl.BlockSpec((D_MODEL, tf), lambda i, j: (0, j)),
                      pl.BlockSpec((tf, D_MODEL), lambda i, j: (j, 0))]
    if convert_next:
        out_shape += [jax.ShapeDtypeStruct(w.shape[2:], BF16) for w in stacked]
        out_specs += [pl.BlockSpec((pr, tf), lambda i, j: (i, j)),
                      pl.BlockSpec((pr, tf), lambda i, j: (i, j)),
                      pl.BlockSpec((pd, D_MODEL), lambda i, j: (down_idx(i, j), 0))]
    scratch = [pltpu.VMEM((2, TM, D_MODEL), F32), pltpu.VMEM((TM, D_MODEL), BF16),
               pltpu.SemaphoreType.DMA((2,)), pltpu.SemaphoreType.DMA((2,))]
    if emit_norm:
        scratch += [pltpu.VMEM((TM, D_MODEL), BF16), pltpu.SemaphoreType.DMA(())]
    if head is not None:
        scratch += [pltpu.SemaphoreType.DMA((2,))]

    res = pl.pallas_call(
        functools.partial(_ffn_kernel, in_rows=in_rows, out_rows=tuple(out_rows), tile0=tile0,
                          n_tiles=n_i, own_f32=own_f32 is not None, convert_next=convert_next,
                          emit_norm=emit_norm, final_norm=final_norm, head_rows=head_rows),
        out_shape=out_shape,
        grid=(n_i, n_j),
        in_specs=in_specs,
        out_specs=out_specs,
        scratch_shapes=scratch,
        compiler_params=_params(("arbitrary", "arbitrary")),
        name="ffn",
    )(*args)
    res = list(res)
    outs = [res.pop(0) for _ in out_rows]
    xn = res.pop(0) if emit_norm else None
    own = [res.pop(0) for _ in range(3)] if own_f32 is not None else None
    return outs, xn, own, (res if convert_next else None)


def _proj_residual_kernel(h_ref, w_ref, x_ref, o_ref, wb_ref):
    @pl.when(pl.program_id(1) == 0)
    def _():
        wb_ref[...] = w_ref[...].astype(BF16)

    o_ref[...] = x_ref[...] + jnp.dot(h_ref[...], wb_ref[...], preferred_element_type=F32)


def _proj_residual(h, w, x):
    rows, k = h.shape
    return pl.pallas_call(
        _proj_residual_kernel,
        out_shape=jax.ShapeDtypeStruct((rows, D_MODEL), F32),
        grid=(D_MODEL // TN, rows // TM),
        in_specs=[
            pl.BlockSpec((TM, k), lambda j, i: (i, 0)),
            pl.BlockSpec((k, TN), lambda j, i: (0, j)),
            pl.BlockSpec((TM, TN), lambda j, i: (i, j)),
        ],
        out_specs=pl.BlockSpec((TM, TN), lambda j, i: (i, j)),
        scratch_shapes=[pltpu.VMEM((k, TN), BF16)],
        compiler_params=_params(("arbitrary", "arbitrary")),
        name="proj_residual",
    )(h, w, x)


def _conv_a(abuf, sbuf, wa_ref, n_rows):
    shift_rows = A_HALO + n_rows - SUBLANES
    for s in range(1, SUBLANES):
        sbuf[s - 1, 0:shift_rows, :] = abuf[s:s + shift_rows, :]

    a0 = A_HALO - (K_A - 1)

    def tap(k, r0, cs):
        q, s = divmod(a0 + k, SUBLANES)
        src = abuf if s == 0 else sbuf.at[s - 1]
        return wa_ref[k:k + 1, cs] * src[SUBLANES * q + r0:SUBLANES * q + r0 + CONV_RB, cs]

    for c in range(MIX_C // LANES):
        cs = slice(c * LANES, (c + 1) * LANES)
        for r0 in range(0, n_rows, CONV_RB):
            acc = tap(0, r0, cs)
            for k in range(1, K_A):
                acc = acc + tap(k, r0, cs)
            yield slice(r0, r0 + CONV_RB), cs, acc


def _load_bf16_columns(w_hbm, col_starts, width, dsts, stage, sem):
    def chunk(c):
        return pltpu.make_async_copy(w_hbm.at[:, pl.ds(col_starts[c], width)], stage.at[c % 2],
                                     sem.at[c % 2])
    chunk(0).start()
    for c in range(len(col_starts)):
        if c + 1 < len(col_starts):
            chunk(c + 1).start()
        chunk(c).wait()
        dsts[c][...] = stage[c % 2].astype(BF16)


def _mixer_in_kernel(xn_ref, w_hbm, wa_ref, ba_ref, wb_ref, sa_ref, sb_ref,
                     ac_ref, hb_ref, nap_ref, nbp_ref, nas_ref, nbs_ref,
                     wbf, abuf, sbuf, bbuf, stage, sem, *, n_prompt_tiles, tiles_per_seq):
    c_tile = pl.program_id(0)
    r = pl.program_id(1)
    n_groups = 5
    wcur = wbf.at[lax.rem(c_tile, 2)]
    wnext = wbf.at[1 - lax.rem(c_tile, 2)]

    @pl.when((c_tile == 0) & (r == 0))
    def _():
        _load_bf16_columns(w_hbm, [g * D_A for g in range(n_groups)], MIX_C,
                           [wcur.at[:, pl.ds(g * MIX_C, MIX_C)] for g in range(n_groups)],
                           stage, sem)

    @pl.when(c_tile + 1 < pl.num_programs(0))
    def _():
        col = pl.multiple_of((c_tile + 1) * MIX_C, MIX_C)

        def fetch(g):
            return pltpu.make_async_copy(w_hbm.at[:, pl.ds(g * D_A + col, MIX_C)],
                                         stage.at[g % 2], sem.at[g % 2])
        for g in range(n_groups):
            @pl.when(r == g + 2)
            def _(g=g):
                fetch(g).wait()
                wnext[:, g * MIX_C:(g + 1) * MIX_C] = stage[g % 2].astype(BF16)

            @pl.when(r == g + 1)
            def _(g=g):
                fetch(g).start()

    def project(xn):
        p = jnp.dot(xn, wcur[...], preferred_element_type=F32)
        return [p[:, g * MIX_C:(g + 1) * MIX_C] for g in range(n_groups)]

    @pl.when(r < n_prompt_tiles)
    def _():
        @pl.when(lax.rem(r, tiles_per_seq) == 0)
        def _():
            abuf[0:A_HALO, :] = jnp.zeros((A_HALO, MIX_C), F32)
            bbuf[0:B_HALO, :] = jnp.zeros((B_HALO, MIX_C), F32)

        pa, ga, bg, cg, hb = project(xn_ref[...])
        abuf[A_HALO:A_HALO + MIX_T, :] = pa * jax.nn.sigmoid(ga)
        bbuf[B_HALO:B_HALO + MIX_T, :] = cg * hb

        for rs, cs, acc in _conv_a(abuf, sbuf, wa_ref, MIX_T):
            ac_ref[rs, cs] = acc + ba_ref[:, cs]

        b0 = B_HALO - (K_B - 1)
        b_c = wb_ref[0:1, :] * bbuf[b0:b0 + MIX_T, :]
        for k in range(1, K_B):
            b_c = b_c + wb_ref[k:k + 1, :] * bbuf[b0 + k:b0 + k + MIX_T, :]
        hb_ref[...] = (bg * b_c).astype(BF16)

        @pl.when(lax.rem(r, tiles_per_seq) == tiles_per_seq - 1)
        def _():
            nap_ref[0] = abuf[MIX_T + A_HALO - (K_A - 1):MIX_T + A_HALO, :]
            nbp_ref[0] = bbuf[MIX_T + b0:MIX_T + B_HALO, :]

        abuf[0:A_HALO, :] = abuf[MIX_T:MIX_T + A_HALO, :]
        bbuf[0:B_HALO, :] = bbuf[MIX_T:MIX_T + B_HALO, :]

    @pl.when(r >= n_prompt_tiles)
    def _():
        rs = pl.ds(pl.multiple_of((r - n_prompt_tiles) * SAMPLE_BS, SAMPLE_BS), SAMPLE_BS)
        pa, ga, bg, cg, hb = project(xn_ref[rs, :])
        a = pa * jax.nn.sigmoid(ga)
        acc = wa_ref[K_A - 1:K_A, :] * a
        for k in range(K_A - 1):
            acc = acc + wa_ref[k:k + 1, :] * sa_ref[k]
        ac_ref[rs, :] = acc + ba_ref[...]
        for k in range(K_A - 2):
            nas_ref[k] = sa_ref[k + 1]
        nas_ref[K_A - 2] = a

        ch = cg * hb
        b_c = wb_ref[K_B - 1:K_B, :] * ch
        for k in range(K_B - 1):
            b_c = b_c + wb_ref[k:k + 1, :] * sb_ref[:, k, :]
        hb_ref[rs, :] = (bg * b_c).astype(BF16)
        for k in range(K_B - 2):
            nbs_ref[:, k, :] = sb_ref[:, k + 1, :]
        nbs_ref[:, K_B - 2, :] = ch


def _mixer_in(xn, w_in, wa, ba, wb, sa, sb, layer, batch, seq):
    rows = xn.shape[0]
    n_sample = sa.shape[2]
    n_prompt_tiles = batch * seq // MIX_T
    tiles_per_seq = seq // MIX_T
    assert rows == batch * seq + n_sample and n_sample <= MIX_T and seq % MIX_T == 0
    assert n_sample % SAMPLE_BS == 0
    assert n_prompt_tiles + n_sample // SAMPLE_BS >= 7
    row_blk = lambda r: jnp.minimum(r, n_prompt_tiles)
    smp_blk = lambda r: jnp.maximum(r - n_prompt_tiles, 0)
    seq_of = lambda r: jnp.minimum(r, n_prompt_tiles - 1) // tiles_per_seq
    chan = lambda k: pl.BlockSpec((k, MIX_C), lambda c, r: (0, c))
    state = lambda k, lyr: pl.BlockSpec((None, SAMPLE_BS, k, MIX_C),
                                        lambda c, r: (lyr, smp_blk(r), 0, c))
    tap_major = lambda k, lyr: pl.BlockSpec((None, k, SAMPLE_BS, MIX_C),
                                            lambda c, r: (lyr, 0, smp_blk(r), c))
    prompt_state = lambda k: pl.BlockSpec((None, 1, k, MIX_C), lambda c, r: (0, seq_of(r), 0, c))
    return pl.pallas_call(
        functools.partial(_mixer_in_kernel, n_prompt_tiles=n_prompt_tiles,
                          tiles_per_seq=tiles_per_seq),
        out_shape=(jax.ShapeDtypeStruct((rows, D_A), F32),
                   jax.ShapeDtypeStruct((rows, D_B), BF16),
                   jax.ShapeDtypeStruct((1, batch, K_A - 1, D_A), F32),
                   jax.ShapeDtypeStruct((1, batch, K_B - 1, D_B), F32),
                   jax.ShapeDtypeStruct((1, K_A - 1, n_sample, D_A), F32),
                   jax.ShapeDtypeStruct((1, n_sample, K_B - 1, D_B), F32)),
        grid=(D_A // MIX_C, n_prompt_tiles + n_sample // SAMPLE_BS),
        in_specs=[pl.BlockSpec((MIX_T, D_MODEL), lambda c, r: (row_blk(r), 0)),
                  pl.BlockSpec(memory_space=pl.ANY),
                  chan(K_A), chan(1), chan(K_B),
                  tap_major(K_A - 1, layer), state(K_B - 1, layer)],
        out_specs=(pl.BlockSpec((MIX_T, MIX_C), lambda c, r: (row_blk(r), c)),
                   pl.BlockSpec((MIX_T, MIX_C), lambda c, r: (row_blk(r), c)),
                   prompt_state(K_A - 1), prompt_state(K_B - 1),
                   tap_major(K_A - 1, 0), state(K_B - 1, 0)),
        scratch_shapes=[pltpu.VMEM((2, D_MODEL, 5 * MIX_C), BF16),
                        pltpu.VMEM((A_HALO + MIX_T, MIX_C), F32),
                        pltpu.VMEM((SUBLANES - 1, A_HALO + MIX_T - SUBLANES, MIX_C), F32),
                        pltpu.VMEM((B_HALO + MIX_T, MIX_C), F32),
                        pltpu.VMEM((2, D_MODEL, MIX_C), F32),
                        pltpu.SemaphoreType.DMA((2,))],
        compiler_params=_params(("arbitrary", "arbitrary")),
        name="mixer_in",
    )(xn, w_in, wa, ba, wb, sa, sb)


def _mixer_out_kernel(ac_ref, hb_ref, lg_ref, lb_ref, w_hbm, x_ref, o_ref, wb_ref, stage, sem):
    j = pl.program_id(0)

    @pl.when(pl.program_id(1) == 0)
    def _():
        col = pl.multiple_of(j * TN, TN)
        offs = range(0, TN, W_STAGE)
        _load_bf16_columns(w_hbm, [col + o for o in offs], W_STAGE,
                           [wb_ref.at[:, pl.ds(o, W_STAGE)] for o in offs], stage, sem)

    a_ln = _layernorm_rows(ac_ref[...], lg_ref[...], lb_ref[...])
    h_a = (a_ln * jax.nn.sigmoid(a_ln)).astype(BF16)
    y = jnp.dot(h_a, wb_ref[0:D_A, :], preferred_element_type=F32)
    y = y + jnp.dot(hb_ref[...], wb_ref[D_A:D_A + D_B, :], preferred_element_type=F32)
    o_ref[...] = x_ref[...] + y


def _mixer_out(ac, hb, lg, lb, w, x):
    rows = x.shape[0]
    return pl.pallas_call(
        _mixer_out_kernel,
        out_shape=jax.ShapeDtypeStruct((rows, D_MODEL), F32),
        grid=(D_MODEL // TN, rows // TM),
        in_specs=[
            pl.BlockSpec((TM, D_A), lambda j, i: (i, 0)),
            pl.BlockSpec((TM, D_B), lambda j, i: (i, 0)),
            pl.BlockSpec((1, D_A), lambda j, i: (0, 0)),
            pl.BlockSpec((1, D_A), lambda j, i: (0, 0)),
            pl.BlockSpec(memory_space=pl.ANY),
            pl.BlockSpec((TM, TN), lambda j, i: (i, j)),
        ],
        out_specs=pl.BlockSpec((TM, TN), lambda j, i: (i, j)),
        scratch_shapes=[pltpu.VMEM((D_A + D_B, TN), BF16),
                        pltpu.VMEM((2, D_A + D_B, W_STAGE), F32),
                        pltpu.SemaphoreType.DMA((2,))],
        compiler_params=_params(("arbitrary", "arbitrary")),
        name="mixer_out",
    )(ac, hb, lg, lb, w, x)


def _gelu(y):
    return 0.5 * y * (1.0 + lax.erf(y * (2.0 ** -0.5)))


def _gmlp_in_kernel(xn_ref, w_hbm, b_ref, lg_ref, lb_ref, ws_ref, bs_ref, o_ref, v_ref,
                    wb_ref, s_ref, stage, sem, *, n_prompt_chunks):
    i = pl.program_id(0)

    @pl.when(i == 0)
    def _():
        offs = list(range(0, 2 * D_C, W_STAGE))
        _load_bf16_columns(w_hbm, offs, W_STAGE,
                           [wb_ref.at[:, pl.ds(o, W_STAGE)] for o in offs], stage, sem)

    xn = xn_ref[...]
    zv = _gelu(jnp.dot(xn, wb_ref[:, D_C:2 * D_C], preferred_element_type=F32)
               + b_ref[:, D_C:2 * D_C])
    row = lax.broadcasted_iota(jnp.int32, (CHUNK, CHUNK), 0)
    col = lax.broadcasted_iota(jnp.int32, (CHUNK, CHUNK), 1)
    for c in range(GATE_CHUNKS):
        rs = slice(c * CHUNK, (c + 1) * CHUNK)
        is_sample = i * GATE_CHUNKS + c >= n_prompt_chunks
        sel = jnp.where(is_sample, 1, 0)
        vn = _layernorm_rows(zv[rs, :], lg_ref[...], lb_ref[...])
        vb = vn.astype(BF16)
        keep = (col <= row) & (col >= jnp.where(is_sample, row, 0))
        for h in range(H_C):
            hs = slice(h * DH_C, (h + 1) * DH_C)
            w = jnp.where(keep, ws_ref[sel, h], 0.0).astype(BF16)
            s_ref[rs, hs] = (jnp.dot(w, vb[:, hs], preferred_element_type=F32)
                             + bs_ref[sel][:, h:h + 1])

        if c == GATE_CHUNKS - 1:
            @pl.when(i == pl.num_programs(0) - 1)
            def _():
                v_ref[...] = vn

    zu = _gelu(jnp.dot(xn, wb_ref[:, 0:D_C], preferred_element_type=F32) + b_ref[:, 0:D_C])
    o_ref[...] = (zu * s_ref[...]).astype(BF16)


def _gmlp_in(xn, w_in, b_in, lg, lb, ws_all, bs_all, n_prompt_chunks):
    rows = xn.shape[0]
    tile = GATE_CHUNKS * CHUNK
    assert rows % tile == 0 and rows // CHUNK == n_prompt_chunks + 1
    return pl.pallas_call(
        functools.partial(_gmlp_in_kernel, n_prompt_chunks=n_prompt_chunks),
        out_shape=(jax.ShapeDtypeStruct((rows, D_C), BF16),
                   jax.ShapeDtypeStruct((CHUNK, D_C), F32)),
        grid=(rows // tile,),
        in_specs=[
            pl.BlockSpec((tile, D_MODEL), lambda i: (i, 0)),
            pl.BlockSpec(memory_space=pl.ANY),
            pl.BlockSpec((1, 2 * D_C), lambda i: (0, 0)),
            pl.BlockSpec((1, D_C), lambda i: (0, 0)),
            pl.BlockSpec((1, D_C), lambda i: (0, 0)),
            pl.BlockSpec((2, H_C, CHUNK, CHUNK), lambda i: (0, 0, 0, 0)),
            pl.BlockSpec((2, CHUNK, H_C), lambda i: (0, 0, 0)),
        ],
        out_specs=(pl.BlockSpec((tile, D_C), lambda i: (i, 0)),
                   pl.BlockSpec((CHUNK, D_C), lambda i: (0, 0))),
        scratch_shapes=[pltpu.VMEM((D_MODEL, 2 * D_C), BF16),
                        pltpu.VMEM((tile, D_C), F32),
                        pltpu.VMEM((2, D_MODEL, W_STAGE), F32),
                        pltpu.SemaphoreType.DMA((2,))],
        compiler_params=_params(("arbitrary",)),
        name="gmlp_in",
    )(xn, w_in, b_in, lg, lb, ws_all, bs_all)


def kernel(x_prompt, x_sample, state_conv_a, state_conv_b, norm_g, ffn_w1, ffn_w3, ffn_w2,
           ab_w_in, a_conv_w, a_conv_b, a_ln_g, a_ln_b, b_conv_w, ab_w_out,
           c_w_in, c_b_in, c_ln_g, c_ln_b, c_w_s, c_b_s, c_w_out, final_g):
    batch, seq, _ = x_prompt.shape
    nb, dec_seq, _ = x_sample.shape
    depth = norm_g.shape[0]
    n_prompt = batch * seq
    rows = n_prompt + nb
    assert dec_seq == 1 and nb == CHUNK and n_prompt % CHUNK == 0

    row = lambda v: v.reshape(1, -1)
    ffn_stacked = (ffn_w1, ffn_w3, ffn_w2)
    w_bf16 = None

    xs =[x_prompt.reshape(n_prompt, D_MODEL), x_sample.reshape(nb, D_MODEL)]
    state_a_taps = jnp.swapaxes(state_conv_a, 1, 2)
    new_a_p, new_b_p, new_a_s, new_b_s, new_v_s = [], [], [], [], []
    for i in range(depth):
        j = i // 2
        head = None
        if w_bf16 is None:
            (x_head,), xn_head, w_bf16, _ = _ffn(
                xs, row(norm_g[i, 0]), None, row(norm_g[i, 1]), (TM,), tf=TF_F32, n_tiles=1,
                own_f32=(ffn_stacked, (i, 0)), emit_norm=True)
            head = (x_head, xn_head)
        (x,), xn, _, w_bf16 = _ffn(xs, row(norm_g[i, 0]), w_bf16, row(norm_g[i, 1]), (rows,),
                                   tile0=0 if head is None else 1, head=head,
                                   next_w=(ffn_stacked, (i, 1)), emit_norm=True)
        if i % 2 == 0:
            ac, hb, sa_p, sb_p, sa_s, sb_s = _mixer_in(
                xn, ab_w_in[j], a_conv_w[j], row(a_conv_b[j]), b_conv_w[j],
                state_a_taps, state_conv_b, j, batch, seq)
            new_a_p.append(sa_p)
            new_b_p.append(sb_p)
            new_a_s.append(jnp.swapaxes(sa_s, 1, 2))
            new_b_s.append(sb_s)
            x = _mixer_out(ac, hb, row(a_ln_g[j]), row(a_ln_b[j]), ab_w_out[j], x)
        else:
            w_s =c_w_s[j][:, :CHUNK, :CHUNK]
            ws_all = jnp.stack([w_s, jnp.broadcast_to(w_s[:, :1, :1], w_s.shape)])
            b_s = c_b_s[j][:, :CHUNK].T
            bs_all = jnp.stack([b_s, jnp.broadcast_to(b_s[:1], b_s.shape)])
            h, v_s = _gmlp_in(xn, c_w_in[j], row(c_b_in[j]), row(c_ln_g[j]), row(c_ln_b[j]),
                              ws_all, bs_all, n_prompt // CHUNK)
            new_v_s.append(v_s.reshape(nb, 1, D_C))
            x = _proj_residual(h, c_w_out[j], x)
        last = i == depth - 1
        xs, _, _, w_bf16 = _ffn([x], row(norm_g[i, 2]), w_bf16, row(final_g),
                                (n_prompt, nb) if last else (rows,),
                                next_w=None if last else (ffn_stacked, (i + 1, 0)),
                                final_norm=last)

    y_prompt, y_sample = xs
    return (y_prompt.reshape(batch, seq, D_MODEL), y_sample.reshape(nb, 1, D_MODEL),
            jnp.concatenate(new_a_p), jnp.concatenate(new_b_p),
            jnp.concatenate(new_a_s), jnp.concatenate(new_b_s),
            jnp.stack(new_v_s))
```
